```python
import math
import jax, jax.numpy as jnp
from jax import lax
import numpy as np

D_MODEL = 1024
BATCH = 4
SEQ = 4096
DEPTH = 1
DEC_BATCH = 32
DEC_SEQ = 8
PAST_LEN = 16384
PAGE_SIZE = 128

N_MEM = 256
H_R = 4
DK_R = 64
DV_R = 128
RET_CHUNK = 128
ROPE_BASE = 10000.0
H_F = 8
D_F = 64
Q_BLOCK = 128
FORGET_BIAS_INIT = 2.0
H_M = 4
D_M = 128
D_FF = 2816
N_BRANCH = 3
LN_EPS = 1e-5
GN_EPS = 1e-5
ALPHA = (2.0 * DEPTH) ** 0.25
BETA = (8.0 * DEPTH) ** -0.25

RQ = H_R * DK_R
RV = H_R * DV_R
FQ = H_F * D_F
MQ = H_M * D_M
SPLIT_SIZES = (RQ, RQ, RV, RV, FQ, FQ, FQ, H_F, MQ, N_BRANCH * D_MODEL)
SPLIT_SCALE = (1.0, 1.0, BETA, 1.0, 1.0, 1.0, BETA, 1.0, 1.0, 1.0)
SPLIT_IDX = tuple(int(i) for i in np.cumsum(SPLIT_SIZES)[:-1])

kernel_name = 'hybrid_retention_fox_memory_macaron_step'

F32 = jnp.float32


def layer_norm(x, g, b):
    xf = x.astype(F32)
    mu = jnp.mean(xf, axis=-1, keepdims=True)
    var = jnp.mean(jnp.square(xf - mu), axis=-1, keepdims=True)
    y = (xf - mu) * lax.rsqrt(var + LN_EPS) * g.astype(F32) + b.astype(F32)
    return y.astype(x.dtype)


def swiglu(x, wg, wu, wd):
    return (jax.nn.silu(x @ wg) * (x @ wu)) @ wd


def rotary(x, pos):
    half = x.shape[-1] // 2
    inv = ROPE_BASE ** (-jnp.arange(half, dtype=F32) / half)
    ang = pos[:, None] * inv[None, :]
    cos = jnp.cos(ang)[None, :, None, :]
    sin = jnp.sin(ang)[None, :, None, :]
    x1 = x[..., :half].astype(F32)
    x2 = x[..., half:].astype(F32)
    return jnp.concatenate([x1 * cos - x2 * sin, x2 * cos + x1 * sin], axis=-1)


def retention(q, k, v, s0):
    B, L, H, dk = q.shape
    dv = v.shape[-1]
    C = math.gcd(L, RET_CHUNK)
    n = L // C
    log_g = jnp.log1p(-jnp.exp2(-5.0 - jnp.arange(H, dtype=F32)))
    i = jnp.arange(C, dtype=F32)
    diff = i[:, None] - i[None, :]
    dmask = jnp.where(diff[None] >= 0, jnp.exp(jnp.maximum(diff, 0.0)[None] * log_g[:, None, None]), 0.0)
    q_dec = jnp.exp((i + 1.0)[:, None] * log_g[None, :])
    k_dec = jnp.exp((C - 1.0 - i)[:, None] * log_g[None, :])
    chunk_dec = jnp.exp(C * log_g)

    def to_chunks(a, d):
        return a.astype(F32).reshape(B, n, C, H, d).transpose(1, 0, 2, 3, 4)

    qc, kc, vc = to_chunks(q, dk), to_chunks(k, dk), to_chunks(v, dv)

    def step(S, inp):
        qb, kb, vb = inp
        a = jnp.einsum('bihd,bjhd->bhij', qb, kb) * dmask[None]
        o = (jnp.einsum('bhij,bjhe->bihe', a, vb)
             + jnp.einsum('bihd,bhde->bihe', qb * q_dec[None, :, :, None], S))
        S = S * chunk_dec[None, :, None, None] + jnp.einsum('bjhd,bjhe->bhde', kb * k_dec[None, :, :, None], vb)
        return S, o

    S, o = lax.scan(step, s0.astype(F32), (qc, kc, vc))
    o = o.transpose(1, 0, 2, 3, 4).reshape(B, L, H, dv)
    return o, S


def head_norm(o, g):
    mu = jnp.mean(o, axis=-1, keepdims=True)
    var = jnp.mean(jnp.square(o - mu), axis=-1, keepdims=True)
    y = (o - mu) * lax.rsqrt(var + GN_EPS)
    B, L, H, dv = o.shape
    return y.reshape(B, L, H * dv) * g.astype(F32)


def fox_attention(q, cq, q_offset, k, v, ck):
    B, Lq, H, D = q.shape
    Lk = k.shape[1]
    QB = math.gcd(Lq, Q_BLOCK)
    nb = Lq // QB
    kpos = jnp.arange(Lk)
    ckT = ck.astype(F32).transpose(0, 2, 1)[:, :, None, :]
    qb = q.reshape(B, nb, QB, H, D).transpose(1, 0, 2, 3, 4)
    cqb = cq.reshape(B, nb, QB, H).transpose(1, 0, 2, 3)
    starts = q_offset + jnp.arange(nb) * QB
    scale = D ** -0.5

    def block(args):
        qi, ci, st = args
        s = jnp.einsum('bqhd,bkhd->bhqk', qi, k, preferred_element_type=F32) * scale
        s = s + ci.astype(F32).transpose(0, 2, 1)[..., None] - ckT
        qpos = st + jnp.arange(QB)
        s = jnp.where(kpos[None, :] <= qpos[:, None], s, -jnp.inf)
        p = jax.nn.softmax(s, axis=-1)
        return jnp.einsum('bhqk,bkhd->bqhd', p.astype(v.dtype), v)

    o = lax.map(block, (qb, cqb, starts))
    return o.transpose(1, 0, 2, 3, 4).reshape(B, Lq, H, D)


def decoder_layer(x, pos, ret_s0, past_k, past_v, past_logf, mem_k, mem_v,
                  ln_g, ln_b, w_ffn_gate, w_ffn_up, w_ffn_down, w_in, b_forget, b_gate,
                  ret_gn_g, w_ret_o, w_fox_o, w_mem_o, w_out):
    B, L, _ = x.shape
    h = layer_norm(ALPHA * x + 0.5 * swiglu(x, w_ffn_gate[0], w_ffn_up[0], w_ffn_down[0]), ln_g[0], ln_b[0])
    proj = h @ w_in
    rq, rk, rv, rg, fq, fk, fv, fl, mq, gl = jnp.split(proj, SPLIT_IDX, axis=-1)

    rq = rotary(rq.reshape(B, L, H_R, DK_R), pos)
    rk = rotary(rk.reshape(B, L, H_R, DK_R), pos) * (DK_R ** -0.5)
    ro, ret_state = retention(rq, rk, rv.reshape(B, L, H_R, DV_R), ret_s0)
    ret_out = (jax.nn.silu(rg) * head_norm(ro, ret_gn_g).astype(rg.dtype)) @ w_ret_o

    fq = fq.reshape(B, L, H_F, D_F)
    fk = fk.reshape(B, L, H_F, D_F)
    fv = fv.reshape(B, L, H_F, D_F)
    logf = jax.nn.log_sigmoid(fl.astype(F32) + b_forget.astype(F32))
    if past_k is None:
        k_all, v_all, lf_all, offset = fk, fv, logf, 0
    else:
        k_all = jnp.concatenate([past_k, fk], axis=1)
        v_all = jnp.concatenate([past_v, fv], axis=1)
        lf_all = jnp.concatenate([past_logf.astype(F32), logf], axis=1)
        offset = past_k.shape[1]
    c_all = lax.cumsum(lf_all, axis=1)
    fo = fox_attention(fq, c_all[:, offset:], offset, k_all, v_all, c_all)
    fox_out = fo.reshape(B, L, FQ).astype(h.dtype) @ w_fox_o

    mqh = mq.reshape(B, L, H_M, D_M)
    s = jnp.einsum('blhd,bmhd->bhlm', mqh, mem_k, preferred_element_type=F32) * (D_M ** -0.5)
    p = jax.nn.softmax(s, axis=-1)
    mo = jnp.einsum('bhlm,bmhd->blhd', p.astype(mem_v.dtype), mem_v)
    mem_out = mo.reshape(B, L, MQ).astype(h.dtype) @ w_mem_o

    g = jax.nn.sigmoid(gl.reshape(B, L, N_BRANCH, D_MODEL) + b_gate)
    mix = (g[:, :, 0] * ret_out + g[:, :, 1] * fox_out + g[:, :, 2] * mem_out) @ w_out
    h = layer_norm(ALPHA * h + mix, ln_g[1], ln_b[1])

    y = layer_norm(ALPHA * h + 0.5 * swiglu(h, w_ffn_gate[1], w_ffn_up[1], w_ffn_down[1]), ln_g[2], ln_b[2])
    return y, fk, fv, logf, ret_state


def setup_inputs(seed: int = 0) -> dict:
    key = jax.random.key(seed)
    ks = jax.random.split(key, 26)

    def nrm(k, shape, scale):
        return jax.random.normal(k, shape, F32) * scale

    n_pages = PAST_LEN // PAGE_SIZE
    n_used = DEC_BATCH * n_pages
    n_pool = n_used + max(1, n_used // 4)
    s_in = D_MODEL ** -0.5

    x_prompt = nrm(ks[0], (BATCH, SEQ, D_MODEL), 1.0)
    x_sample = nrm(ks[1], (DEC_BATCH, DEC_SEQ, D_MODEL), 1.0)
    mem_prompt = nrm(ks[2], (BATCH, N_MEM, D_MODEL), 1.0)
    cache_fox_k = nrm(ks[3], (n_pool, PAGE_SIZE, H_F, D_F), 1.0)
    cache_fox_v = nrm(ks[4], (n_pool, PAGE_SIZE, H_F, D_F), 1.0)
    cache_fox_logf = jax.nn.log_sigmoid(FORGET_BIAS_INIT + nrm(ks[5], (n_pool, PAGE_SIZE, H_F), 1.0))
    state_ret = nrm(ks[6], (DEC_BATCH, H_R, DK_R, DV_R), 1.0)
    cache_mem_k = nrm(ks[7], (DEC_BATCH, N_MEM, H_M, D_M), 1.0)
    cache_mem_v = nrm(ks[8], (DEC_BATCH, N_MEM, H_M, D_M), 1.0)
    page_table = jax.random.permutation(ks[9], n_pool)[:n_used].reshape(DEC_BATCH, n_pages).astype(jnp.int32)

    ln_g = 1.0 + nrm(ks[10], (3, D_MODEL), 0.01)
    ln_b = nrm(ks[11], (3, D_MODEL), 0.01)
    w_ffn_gate = nrm(ks[12], (2, D_MODEL, D_FF), s_in)
    w_ffn_up = nrm(ks[13], (2, D_MODEL, D_FF), s_in)
    w_ffn_down = nrm(ks[14], (2, D_FF, D_MODEL), (D_FF ** -0.5) * BETA)
    col_keys = jax.random.split(ks[15], len(SPLIT_SIZES))
    w_in = jnp.concatenate([nrm(kk, (D_MODEL, n), s_in * sc)
                            for kk, n, sc in zip(col_keys, SPLIT_SIZES, SPLIT_SCALE)], axis=1)
    b_forget = FORGET_BIAS_INIT + nrm(ks[16], (H_F,), 0.1)
    b_gate = nrm(ks[17], (N_BRANCH, D_MODEL), 0.01)
    ret_gn_g = 1.0 + nrm(ks[18], (RV,), 0.01)
    w_ret_o = nrm(ks[19], (RV, D_MODEL), (RV ** -0.5) * BETA)
    w_fox_o = nrm(ks[20], (FQ, D_MODEL), (FQ ** -0.5) * BETA)
    w_mem_k = nrm(ks[21], (D_MODEL, MQ), s_in)
    w_mem_v = nrm(ks[22], (D_MODEL, MQ), s_in * BETA)
    w_mem_o = nrm(ks[23], (MQ, D_MODEL), (MQ ** -0.5) * BETA)
    w_out = nrm(ks[24], (D_MODEL, D_MODEL), s_in * BETA)
    return {'x_prompt': x_prompt, 'x_sample': x_sample, 'mem_prompt': mem_prompt,
            'cache_fox_k': cache_fox_k, 'cache_fox_v': cache_fox_v, 'cache_fox_logf': cache_fox_logf,
            'state_ret': state_ret, 'cache_mem_k': cache_mem_k, 'cache_mem_v': cache_mem_v,
            'page_table': page_table,
            'ln_g': ln_g, 'ln_b': ln_b, 'w_ffn_gate': w_ffn_gate, 'w_ffn_up': w_ffn_up,
            'w_ffn_down': w_ffn_down, 'w_in': w_in, 'b_forget': b_forget, 'b_gate': b_gate,
            'ret_gn_g': ret_gn_g, 'w_ret_o': w_ret_o, 'w_fox_o': w_fox_o, 'w_mem_k': w_mem_k,
            'w_mem_v': w_mem_v, 'w_mem_o': w_mem_o, 'w_out': w_out}


def reference(x_prompt, x_sample, mem_prompt, cache_fox_k, cache_fox_v, cache_fox_logf, state_ret,
              cache_mem_k, cache_mem_v, page_table, ln_g, ln_b, w_ffn_gate, w_ffn_up, w_ffn_down,
              w_in, b_forget, b_gate, ret_gn_g, w_ret_o, w_fox_o, w_mem_k, w_mem_v, w_mem_o, w_out):
    bp, lp, _ = x_prompt.shape
    mem_k_p = (mem_prompt @ w_mem_k).reshape(bp, N_MEM, H_M, D_M)
    mem_v_p = (mem_prompt @ w_mem_v).reshape(bp, N_MEM, H_M, D_M)
    ret0 = jnp.zeros((bp, H_R, DK_R, DV_R), F32)
    pos_p = jnp.arange(lp, dtype=F32)
    y_p, fk_p, fv_p, lf_p, ret_p = decoder_layer(
        x_prompt, pos_p, ret0, None, None, None, mem_k_p, mem_v_p,
        ln_g, ln_b, w_ffn_gate, w_ffn_up, w_ffn_down, w_in, b_forget, b_gate,
        ret_gn_g, w_ret_o, w_fox_o, w_mem_o, w_out)

    db, n_pages = page_table.shape
    past_len = n_pages * PAGE_SIZE
    past_k = cache_fox_k[page_table].reshape(db, past_len, H_F, D_F)
    past_v = cache_fox_v[page_table].reshape(db, past_len, H_F, D_F)
    past_lf = cache_fox_logf[page_table].reshape(db, past_len, H_F)
    pos_s = past_len + jnp.arange(x_sample.shape[1], dtype=F32)
    y_s, fk_s, fv_s, lf_s, ret_s = decoder_layer(
        x_sample, pos_s, state_ret, past_k, past_v, past_lf, cache_mem_k, cache_mem_v,
        ln_g, ln_b, w_ffn_gate, w_ffn_up, w_ffn_down, w_in, b_forget, b_gate,
        ret_gn_g, w_ret_o, w_fox_o, w_mem_o, w_out)

    return (y_p, y_s, fk_p, fv_p, lf_p, ret_p, mem_k_p, mem_v_p, fk_s, fv_s, lf_s, ret_s)
```

```python
import functools
import math

import jax
import jax.numpy as jnp
import numpy as np
from jax import lax
from jax.experimental import pallas as pl
from jax.experimental.pallas import tpu as pltpu

F32 = jnp.float32
BF16 = jnp.bfloat16

DEPTH = 1
H_R, DK_R, DV_R = 4, 64, 128
RET_CHUNK = 128
ROPE_BASE = 10000.0
H_F, D_F = 8, 64
H_M, D_M = 4, 128
N_BRANCH = 3
LN_EPS = 1e-5
GN_EPS = 1e-5
ALPHA = (2.0 * DEPTH) ** 0.25
RQ, RV, FQ, MQ = H_R * DK_R, H_R * DV_R, H_F * D_F, H_M * D_M

LANES = 128
BF16_ROWS = 16
TOKEN_TILE = 256
ATTN_TILE = 512
PAGES_PER_STEP = 8
NEG_BIG = -1e30

_NT = (((1,), (1,)), ((), ()))
_TN = (((0,), (0,)), ((), ()))


def _resident(shape):
    return pl.BlockSpec(shape, lambda *_: (0,) * len(shape), pipeline_mode=pl.Buffered(1))


def _layer_norm(y, g, b):
    mu = jnp.mean(y, axis=-1, keepdims=True)
    yc = y - mu
    var = jnp.mean(yc * yc, axis=-1, keepdims=True)
    return yc * lax.rsqrt(var + LN_EPS) * g + b


def _silu(x):
    return x * jax.nn.sigmoid(x)


def _ffn_ln_body(x_ref, wg_ref, wu_ref, wd_ref, g_ref, b_ref, o_ref):
    x = x_ref[...]
    xb = x.astype(BF16)
    gate = jnp.dot(xb, wg_ref[...], preferred_element_type=F32)
    up = jnp.dot(xb, wu_ref[...], preferred_element_type=F32)
    act = (_silu(gate) * up).astype(BF16)
    f = jnp.dot(act, wd_ref[...], preferred_element_type=F32)
    o_ref[...] = _layer_norm(ALPHA * x + 0.5 * f, g_ref[...], b_ref[...])


def _ffn_ln(x, wg, wu, wd, g, b):
    n, d = x.shape
    d_ff = wg.shape[1]
    tm = min(TOKEN_TILE, n)
    return pl.pallas_call(
        _ffn_ln_body,
        grid=(n // tm,),
        in_specs=[pl.BlockSpec((tm, d), lambda i: (i, 0)),
                  _resident((d, d_ff)), _resident((d, d_ff)), _resident((d_ff, d)),
                  _resident((1, d)), _resident((1, d))],
        out_specs=pl.BlockSpec((tm, d), lambda i: (i, 0)),
        out_shape=jax.ShapeDtypeStruct((n, d), F32),
        name="ffn_ln",
    )(x, wg, wu, wd, g, b)


def _rotary(x, cos, sin_signed):
    width = x.shape[-1]
    half = DK_R // 2
    lane = lax.broadcasted_iota(jnp.int32, x.shape, 1)
    other = jnp.where((lane % DK_R) < half,
                      pltpu.roll(x, width - half, 1),
                      pltpu.roll(x, half, 1))
    return x * cos + other * sin_signed


def _log_sigmoid(x):
    return jnp.minimum(x, 0.0) - jnp.log1p(jnp.exp(-jnp.abs(x)))


def _in_proj_body(h_ref, cos_ref, sin_ref, w_rq, w_rk, w_rv, w_rg, w_fq, w_fk, w_fv, w_fl, w_mq, w_gl,
                  bf_ref, rq_o, rk_o, rv_o, rg_o, fq_o, fk_o, fv_o, lf_o, mq_o, gl_o, *, kv_transposed):
    hb = h_ref[...].astype(BF16)

    def proj(w_ref):
        return jnp.dot(hb, w_ref[...], preferred_element_type=F32)

    def proj_t(wt_ref):
        return lax.dot_general(wt_ref[...], hb, _NT, preferred_element_type=F32)

    cos, sin = cos_ref[...], sin_ref[...]
    rq_o[...] = _rotary(proj(w_rq), cos, sin)
    rk_o[...] = _rotary(proj(w_rk), cos, sin) * (DK_R ** -0.5)
    rv_o[...] = proj(w_rv).astype(rv_o.dtype)
    rg_o[...] = proj(w_rg)
    fq_o[...] = (proj(w_fq) * (D_F ** -0.5)).astype(fq_o.dtype)
    if kv_transposed:
        fk_o[0] = proj_t(w_fk)
        fv_o[0] = proj_t(w_fv)
    else:
        fk_o[...] = proj(w_fk)
        fv_o[...] = proj(w_fv)
    lf_o[0] = _log_sigmoid(proj_t(w_fl) + bf_ref[...])
    mq_o[...] = proj(w_mq).astype(mq_o.dtype)
    gl_o[...] = proj(w_gl)


def _in_proj(h, cos, sin, w, b_forget, groups, kv_transposed, act_dtype):
    n, d = h.shape
    lg = n // groups
    tm = min(TOKEN_TILE, lg)
    nt = lg // tm
    row = lambda c: pl.BlockSpec((tm, c), lambda g, i: (g * nt + i, 0))
    tab = pl.BlockSpec((tm, RQ), lambda g, i: (i, 0))
    colt = lambda c: pl.BlockSpec((1, c, tm), lambda g, i: (g, 0, i))
    wspec = lambda a: _resident(a.shape)
    f32 = lambda c: jax.ShapeDtypeStruct((n, c), F32)
    b16 = lambda c: jax.ShapeDtypeStruct((n, c), act_dtype)
    if kv_transposed:
        kv_spec, kv_shape = colt(FQ), jax.ShapeDtypeStruct((groups, FQ, lg), F32)
    else:
        kv_spec, kv_shape = row(FQ), f32(FQ)
    gates = w["gl"].shape[1]
    return pl.pallas_call(
        functools.partial(_in_proj_body, kv_transposed=kv_transposed),
        grid=(groups, nt),
        in_specs=[row(d), tab, tab] + [wspec(w[k]) for k in
                                       ("rq", "rk", "rv", "rg", "fq", "fk", "fv", "fl", "mq", "gl")]
                 + [_resident((H_F, 1))],
        out_specs=[row(RQ), row(RQ), row(RV), row(RV), row(FQ), kv_spec, kv_spec, colt(H_F), row(MQ),
                   row(gates)],
        out_shape=[f32(RQ), f32(RQ), b16(RV), f32(RV), b16(FQ), kv_shape, kv_shape,
                   jax.ShapeDtypeStruct((groups, H_F, lg), F32), b16(MQ), f32(gates)],
        name="in_proj",
    )(h, cos, sin, w["rq"], w["rk"], w["rv"], w["rg"], w["fq"], w["fk"], w["fv"], w["fl"], w["mq"], w["gl"],
      b_forget)


def _retention_body(rq_ref, rk_ref, rv_ref, rg_ref, s0_ref, dmask_ref, qdec_ref, kdec_ref, gn_ref,
                    o_ref, sfin_ref, s_scr, *, chunk_decay):
    c = pl.program_id(1)

    @pl.when(c == 0)
    def _():
        s_scr[...] = s0_ref[0]

    q = rq_ref[...]
    k = rk_ref[...]
    qd = q * qdec_ref[...]
    kd = k * kdec_ref[...]
    outs = []
    for h in range(H_R):
        ks = slice(h * DK_R, (h + 1) * DK_R)
        vs = slice(h * DV_R, (h + 1) * DV_R)
        vh = rv_ref[:, vs].astype(BF16)
        a = lax.dot_general(q[:, ks].astype(BF16), k[:, ks].astype(BF16), _NT,
                            preferred_element_type=F32) * dmask_ref[h]
        s_h = s_scr[h]
        o = (jnp.dot(a.astype(BF16), vh, preferred_element_type=F32)
             + jnp.dot(qd[:, ks].astype(BF16), s_h.astype(BF16), preferred_element_type=F32))
        s_scr[h] = s_h * chunk_decay[h] + lax.dot_general(kd[:, ks].astype(BF16), vh, _TN,
                                                          preferred_element_type=F32)
        mu = jnp.mean(o, axis=-1, keepdims=True)
        oc = o - mu
        var = jnp.mean(oc * oc, axis=-1, keepdims=True)
        outs.append(_silu(rg_ref[:, vs]) * (oc * lax.rsqrt(var + GN_EPS) * gn_ref[:, vs]))
    o_ref[...] = jnp.concatenate(outs, axis=-1).astype(o_ref.dtype)

    @pl.when(c == pl.num_programs(1) - 1)
    def _():
        sfin_ref[0] = s_scr[...]


def _retention_tables(chunk):
    log_g = jnp.log1p(-jnp.exp2(-5.0 - jnp.arange(H_R, dtype=F32)))
    i = jnp.arange(chunk, dtype=F32)
    diff = i[:, None] - i[None, :]
    dmask = jnp.where(diff[None] >= 0, jnp.exp(jnp.maximum(diff, 0.0)[None] * log_g[:, None, None]), 0.0)
    q_dec = jnp.exp((i + 1.0)[:, None] * log_g[None, :])
    k_dec = jnp.exp((chunk - 1.0 - i)[:, None] * log_g[None, :])
    widen = lambda t: jnp.repeat(t, DK_R, axis=1)
    return dmask, widen(q_dec), widen(k_dec)


def _chunk_decay(chunk):
    log_g = np.log1p(-np.exp2(-5.0 - np.arange(H_R, dtype=np.float32))).astype(np.float32)
    return tuple(float(v) for v in np.exp(np.float32(chunk) * log_g).astype(np.float32))


def _retention(rq, rk, rv, rg, s0, gn_g):
    batch = s0.shape[0]
    n = rq.shape[0]
    length = n // batch
    chunk = math.gcd(length, RET_CHUNK)
    nc = length // chunk
    dmask, q_dec, k_dec = _retention_tables(chunk)
    row = lambda c: pl.BlockSpec((chunk, c), lambda b, i: (b * nc + i, 0))
    state = pl.BlockSpec((1, H_R, DK_R, DV_R), lambda b, i: (b, 0, 0, 0))
    return pl.pallas_call(
        functools.partial(_retention_body, chunk_decay=_chunk_decay(chunk)),
        grid=(batch, nc),
        in_specs=[row(RQ), row(RQ), row(RV), row(RV), state,
                  _resident((H_R, chunk, chunk)), _resident((chunk, RQ)), _resident((chunk, RQ)),
                  _resident((1, RV))],
        out_specs=[row(RV), state],
        out_shape=[jax.ShapeDtypeStruct((n, RV), rv.dtype),
                   jax.ShapeDtypeStruct((batch, H_R, DK_R, DV_R), F32)],
        scratch_shapes=[pltpu.VMEM((H_R, DK_R, DV_R), F32)],
        name="retention",
    )(rq, rk, rv, rg, s0, dmask, q_dec, k_dec, gn_g)


def _lane_prefix_sum(x):
    lane = lax.broadcasted_iota(jnp.int32, x.shape, 1)
    d = 1
    while d < LANES:
        x = x + jnp.where(lane >= d, pltpu.roll(x, d, 1), 0.0)
        d *= 2
    return x


def _lane_suffix_sum(x):
    lane = lax.broadcasted_iota(jnp.int32, x.shape, 1)
    d = 1
    while d < LANES:
        x = x + jnp.where(lane < LANES - d, pltpu.roll(x, LANES - d, 1), 0.0)
        d *= 2
    return x


def _cumsum_body(x_ref, o_ref):
    carry = jnp.zeros((x_ref.shape[1], 1), F32)
    for j in range(x_ref.shape[2] // LANES):
        sl = slice(j * LANES, (j + 1) * LANES)
        c = _lane_prefix_sum(x_ref[0, :, sl]) + carry
        o_ref[0, :, sl] = c
        carry = c[:, LANES - 1:LANES]


def _cumsum_lanes(x):
    spec = pl.BlockSpec((1,) + x.shape[1:], lambda b: (b, 0, 0))
    return pl.pallas_call(_cumsum_body, grid=(x.shape[0],), in_specs=[spec], out_specs=spec,
                          out_shape=jax.ShapeDtypeStruct(x.shape, F32), name="logf_cumsum")(x)


def _fox_prefill_body(q_ref, kt_ref, vt_ref, ck_ref, cq_ref, o_ref, m_scr, l_scr, acc_scr, *, tile):
    qi = pl.program_id(2)
    ki = pl.program_id(3)

    @pl.when(ki == 0)
    def _():
        m_scr[...] = jnp.full(m_scr.shape, NEG_BIG, F32)
        l_scr[...] = jnp.zeros(l_scr.shape, F32)
        acc_scr[...] = jnp.zeros(acc_scr.shape, F32)

    def step(masked):
        for j in range(2):
            ds = slice(j * D_F, (j + 1) * D_F)
            s = jnp.dot(q_ref[:, ds], kt_ref[0, ds, :].astype(BF16), preferred_element_type=F32)
            c_end = cq_ref[0, 0, j:j + 1, tile - 1:tile]
            s = s + (c_end - ck_ref[0, 0, j:j + 1, :])
            if masked:
                row = lax.broadcasted_iota(jnp.int32, s.shape, 0)
                col = lax.broadcasted_iota(jnp.int32, s.shape, 1)
                s = jnp.where(col <= row, s, NEG_BIG)
            m_old = m_scr[j]
            m_new = jnp.maximum(m_old, jnp.max(s, axis=-1, keepdims=True))
            alpha = jnp.exp(m_old - m_new)
            p = jnp.exp(s - m_new)
            l_scr[j] = alpha * l_scr[j] + jnp.sum(p, axis=-1, keepdims=True)
            acc_scr[j] = alpha * acc_scr[j] + lax.dot_general(
                p.astype(BF16), vt_ref[0, ds, :].astype(BF16), _NT, preferred_element_type=F32)
            m_scr[j] = m_new

    @pl.when(ki < qi)
    def _():
        step(False)

    @pl.when(ki == qi)
    def _():
        step(True)

    @pl.when(ki == pl.num_programs(3) - 1)
    def _():
        o_ref[...] = jnp.concatenate([acc_scr[j] / l_scr[j] for j in range(2)], axis=-1).astype(BF16)


def _fox_prefill(fq, fkt, fvt, c):
    batch, _, length = fkt.shape
    tile = min(ATTN_TILE, length)
    nt = length // tile
    pairs = H_F // 2
    c4 = c.reshape(batch, pairs, 2, length)
    kv_spec = pl.BlockSpec((1, 2 * D_F, tile), lambda b, hp, qi, ki: (b, hp, jnp.minimum(ki, qi)))
    return pl.pallas_call(
        functools.partial(_fox_prefill_body, tile=tile),
        grid=(batch, pairs, nt, nt),
        in_specs=[pl.BlockSpec((tile, 2 * D_F), lambda b, hp, qi, ki: (b * nt + qi, hp)),
                  kv_spec, kv_spec,
                  pl.BlockSpec((1, 1, 2, tile), lambda b, hp, qi, ki: (b, hp, 0, jnp.minimum(ki, qi))),
                  pl.BlockSpec((1, 1, 2, tile), lambda b, hp, qi, ki: (b, hp, 0, qi))],
        out_specs=pl.BlockSpec((tile, 2 * D_F), lambda b, hp, qi, ki: (b * nt + qi, hp)),
        out_shape=jax.ShapeDtypeStruct(fq.shape, BF16),
        scratch_shapes=[pltpu.VMEM((2, tile, 1), F32), pltpu.VMEM((2, tile, 1), F32),
                        pltpu.VMEM((2, tile, D_F), F32)],
        name="fox_prefill",
    )(fq, fkt, fvt, c4, c4)


def _fox_decode_body(pt_ref, q_ref, kn_ref, vn_ref, lfn_ref, *rest, pages, t_new):
    k_refs = rest[:pages]
    v_refs = rest[pages:2 * pages]
    lf_refs = rest[2 * pages:3 * pages]
    o_ref = rest[3 * pages]
    qbd_scr, m_scr, l_scr, acc_scr, r_scr = rest[3 * pages + 1:]
    del pt_ref
    p_idx = pl.program_id(1)
    rows = H_F * t_new

    def expand(x):
        return jnp.broadcast_to(x[:, None, :], (H_F, t_new, x.shape[-1])).reshape(rows, x.shape[-1])

    @pl.when(p_idx == 0)
    def _():
        q = jnp.concatenate([q_ref[...].astype(F32)] * H_F, axis=0)
        r_head = lax.broadcasted_iota(jnp.int32, (rows, FQ), 0) // t_new
        l_head = lax.broadcasted_iota(jnp.int32, (rows, FQ), 1) // D_F
        qbd = jnp.where(r_head == l_head, q, 0.0).astype(BF16)
        qbd_scr[...] = qbd
        lfn = lfn_ref[0]
        lane = lax.broadcasted_iota(jnp.int32, lfn.shape, 1)
        cnew = jnp.zeros_like(lfn)
        for i in range(t_new):
            cnew = cnew + jnp.where(lane >= i, lfn[:, i:i + 1], 0.0)
        s = lax.dot_general(qbd, kn_ref[...].astype(BF16), _NT, preferred_element_type=F32) - expand(cnew)
        t_q = lax.broadcasted_iota(jnp.int32, s.shape, 0) % t_new
        t_k = lax.broadcasted_iota(jnp.int32, s.shape, 1)
        s = jnp.where(t_k <= t_q, s, NEG_BIG)
        m = jnp.max(s, axis=-1, keepdims=True)
        p = jnp.exp(s - m)
        m_scr[...] = m
        l_scr[...] = jnp.sum(p, axis=-1, keepdims=True)
        acc_scr[...] = jnp.dot(p.astype(BF16), vn_ref[...].astype(BF16), preferred_element_type=F32)
        r_scr[...] = jnp.zeros(r_scr.shape, F32)

    qbd = qbd_scr[...]
    run = r_scr[...]
    scores = []
    for j in range(pages):
        lf = lf_refs[j][0]
        suffix = _lane_suffix_sum(lf)
        scores.append(jnp.dot(qbd, k_refs[j][0].reshape(FQ, LANES).astype(BF16), preferred_element_type=F32)
                      + expand(suffix - lf + run))
        run = run + suffix[:, 0:1]
    r_scr[...] = run
    s = jnp.concatenate(scores, axis=-1)
    m_old = m_scr[...]
    m_new = jnp.maximum(m_old, jnp.max(s, axis=-1, keepdims=True))
    alpha = jnp.exp(m_old - m_new)
    p = jnp.exp(s - m_new)
    l_scr[...] = alpha * l_scr[...] + jnp.sum(p, axis=-1, keepdims=True)
    p = p.astype(BF16)
    acc = alpha * acc_scr[...]
    for j in range(pages):
        acc = acc + lax.dot_general(p[:, j * LANES:(j + 1) * LANES],
                                    v_refs[j][0].reshape(FQ, LANES).astype(BF16), _NT,
                                    preferred_element_type=F32)
    acc_scr[...] = acc
    m_scr[...] = m_new

    @pl.when(p_idx == pl.num_programs(1) - 1)
    def _():
        full = acc_scr[...] / l_scr[...]
        l_head = lax.broadcasted_iota(jnp.int32, (t_new, FQ), 1) // D_F
        out = jnp.zeros((t_new, FQ), F32)
        for h in range(H_F):
            out = out + jnp.where(l_head == h, full[h * t_new:(h + 1) * t_new, :], 0.0)
        o_ref[...] = out.astype(o_ref.dtype)


def _fox_decode(fq, fk, fv, lf_new, cache_kt, cache_vt, cache_lft, page_table):
    db, n_pages = page_table.shape
    t_new = fq.shape[0] // db
    pages = math.gcd(PAGES_PER_STEP, n_pages)
    steps = n_pages // pages
    rows = H_F * t_new

    def page_spec(j, block):
        zeros = (0,) * (len(block) - 1)
        return pl.BlockSpec(block, lambda b, p, pt: (pt[b, n_pages - 1 - (p * pages + j)],) + zeros)

    tok = pl.BlockSpec((t_new, FQ), lambda b, p, pt: (b, 0))
    grid_spec = pltpu.PrefetchScalarGridSpec(
        num_scalar_prefetch=1,
        grid=(db, steps),
        in_specs=[tok, tok, tok, pl.BlockSpec((1, H_F, t_new), lambda b, p, pt: (b, 0, 0))]
                 + [page_spec(j, (1, H_F, D_F, LANES)) for j in range(pages)]
                 + [page_spec(j, (1, H_F, D_F, LANES)) for j in range(pages)]
                 + [page_spec(j, (1, H_F, LANES)) for j in range(pages)],
        out_specs=tok,
        scratch_shapes=[pltpu.VMEM((rows, FQ), BF16), pltpu.VMEM((rows, 1), F32), pltpu.VMEM((rows, 1), F32),
                        pltpu.VMEM((rows, FQ), F32), pltpu.VMEM((H_F, 1), F32)],
    )
    return pl.pallas_call(
        functools.partial(_fox_decode_body, pages=pages, t_new=t_new),
        grid_spec=grid_spec,
        out_shape=jax.ShapeDtypeStruct(fq.shape, fq.dtype),
        name="fox_decode",
    )(page_table, fq, fk, fv, lf_new, *([cache_kt] * pages), *([cache_vt] * pages), *([cache_lft] * pages))


def _mem_attn_body(q_ref, k_ref, v_ref, o_ref):
    outs = []
    for h in range(H_M):
        sl = slice(h * D_M, (h + 1) * D_M)
        s = lax.dot_general(q_ref[:, sl].astype(BF16), k_ref[0, :, sl].astype(BF16), _NT,
                            preferred_element_type=F32) * (D_M ** -0.5)
        p = jnp.exp(s - jnp.max(s, axis=-1, keepdims=True))
        l = jnp.sum(p, axis=-1, keepdims=True)
        outs.append(jnp.dot(p.astype(BF16), v_ref[0, :, sl].astype(BF16), preferred_element_type=F32) / l)
    o_ref[...] = jnp.concatenate(outs, axis=-1).astype(o_ref.dtype)


def _mem_attn(mq, mem_k, mem_v):
    batch, n_mem, _ = mem_k.shape
    length = mq.shape[0] // batch
    tq = min(ATTN_TILE, length)
    nt = length // tq
    row = pl.BlockSpec((tq, MQ), lambda b, i: (b * nt + i, 0))
    mem = pl.BlockSpec((1, n_mem, MQ), lambda b, i: (b, 0, 0))
    return pl.pallas_call(_mem_attn_body, grid=(batch, nt), in_specs=[row, mem, mem], out_specs=row,
                          out_shape=jax.ShapeDtypeStruct(mq.shape, mq.dtype), name="mem_attn")(mq, mem_k, mem_v)


def _matmul_body(x_ref, w_ref, o_ref):
    o_ref[...] = jnp.dot(x_ref[...].astype(BF16), w_ref[...], preferred_element_type=F32)


def _matmul(x, w):
    n, d = x.shape
    tm = min(TOKEN_TILE, n)
    return pl.pallas_call(
        _matmul_body, grid=(n // tm,),
        in_specs=[pl.BlockSpec((tm, d), lambda i: (i, 0)), _resident(w.shape)],
        out_specs=pl.BlockSpec((tm, w.shape[1]), lambda i: (i, 0)),
        out_shape=jax.ShapeDtypeStruct((n, w.shape[1]), F32), name="mem_kv_proj")(x, w)


def _merge_body(h_ref, ret_ref, fox_ref, mem_ref, gl_ref, w_ret, w_fox, w_mem, w_out, bg_ref, g_ref, b_ref,
                o_ref):
    d = h_ref.shape[1]
    mix = jnp.zeros(h_ref.shape, F32)
    for i, (x_ref, w_ref) in enumerate(((ret_ref, w_ret), (fox_ref, w_fox), (mem_ref, w_mem))):
        gate = jax.nn.sigmoid(gl_ref[:, i * d:(i + 1) * d] + bg_ref[i:i + 1, :])
        mix = mix + gate * jnp.dot(x_ref[...].astype(BF16), w_ref[...], preferred_element_type=F32)
    mixed = jnp.dot(mix.astype(BF16), w_out[...], preferred_element_type=F32)
    o_ref[...] = _layer_norm(ALPHA * h_ref[...] + mixed, g_ref[...], b_ref[...])


def _merge(h, ret, fox, mem, gl, w_ret, w_fox, w_mem, w_out, b_gate, g, b):
    n, d = h.shape
    tm = min(TOKEN_TILE, n)
    row = lambda c: pl.BlockSpec((tm, c), lambda i: (i, 0))
    return pl.pallas_call(
        _merge_body, grid=(n // tm,),
        in_specs=[row(d), row(RV), row(FQ), row(MQ), row(N_BRANCH * d),
                  _resident(w_ret.shape), _resident(w_fox.shape), _resident(w_mem.shape), _resident(w_out.shape),
                  _resident(b_gate.shape), _resident((1, d)), _resident((1, d))],
        out_specs=row(d),
        out_shape=jax.ShapeDtypeStruct((n, d), F32), name="merge",
    )(h, ret, fox, mem, gl, w_ret, w_fox, w_mem, w_out, b_gate, g, b)


def _rope_tables(pos):
    half = DK_R // 2
    inv = ROPE_BASE ** (-jnp.arange(half, dtype=F32) / half)
    ang = pos[:, None] * inv[None, :]
    cos, sin = jnp.cos(ang), jnp.sin(ang)
    return (jnp.tile(jnp.concatenate([cos, cos], axis=1), (1, H_R)),
            jnp.tile(jnp.concatenate([-sin, sin], axis=1), (1, H_R)))


def _split_w_in(w_in, d_model, kv_transposed):
    sizes = (RQ, RQ, RV, RV, FQ, FQ, FQ, H_F, MQ, N_BRANCH * d_model)
    names = ("rq", "rk", "rv", "rg", "fq", "fk", "fv", "fl", "mq", "gl")
    offs = np.concatenate([[0], np.cumsum(sizes)])
    w = {nm: w_in[:, int(offs[i]):int(offs[i + 1])].astype(BF16) for i, nm in enumerate(names)}
    w["fl"] = w["fl"].T
    if kv_transposed:
        w["fk"], w["fv"] = w["fk"].T, w["fv"].T
    return w


def _layer(x, pos, groups, s0, mem_k, mem_v, fox_attend, kv_transposed, p):
    d = x.shape[1]
    ln_g, ln_b = p["ln_g"], p["ln_b"]
    act_dtype = BF16 if (x.shape[0] // s0.shape[0]) % BF16_ROWS == 0 else F32
    h = _ffn_ln(x, p["wg"][0], p["wu"][0], p["wd"][0], ln_g[0:1], ln_b[0:1])
    cos, sin = _rope_tables(pos)
    rq, rk, rv, rg, fq, fk, fv, lf, mq, gl = _in_proj(
        h, cos, sin, _split_w_in(p["w_in"], d, kv_transposed), p["b_forget"], groups, kv_transposed, act_dtype)
    ret, ret_state = _retention(rq, rk, rv, rg, s0, p["ret_gn_g"])
    fox = fox_attend(fq, fk, fv, lf)
    mem = _mem_attn(mq, mem_k, mem_v)
    h = _merge(h, ret, fox, mem, gl, p["w_ret_o"], p["w_fox_o"], p["w_mem_o"], p["w_out"], p["b_gate"],
               ln_g[1:2], ln_b[1:2])
    y = _ffn_ln(h, p["wg"][1], p["wu"][1], p["wd"][1], ln_g[2:3], ln_b[2:3])
    return y, fk, fv, lf, ret_state


def kernel(x_prompt, x_sample, mem_prompt, cache_fox_k, cache_fox_v, cache_fox_logf, state_ret, cache_mem_k, cache_mem_v, page_table, ln_g, ln_b, w_ffn_gate, w_ffn_up, w_ffn_down, w_in, b_forget, b_gate, ret_gn_g, w_ret_o, w_fox_o, w_mem_k, w_mem_v, w_mem_o, w_out):
    bp, lp, d = x_prompt.shape
    db, ls, _ = x_sample.shape
    n_mem = mem_prompt.shape[1]
    n_pages = page_table.shape[1]
    page = cache_fox_k.shape[1]
    past_len = n_pages * page
    params = {
        "ln_g": ln_g, "ln_b": ln_b,
        "wg": w_ffn_gate.astype(BF16), "wu": w_ffn_up.astype(BF16), "wd": w_ffn_down.astype(BF16),
        "w_in": w_in, "b_forget": b_forget.reshape(H_F, 1), "b_gate": b_gate,
        "ret_gn_g": ret_gn_g.reshape(1, RV),
        "w_ret_o": w_ret_o.astype(BF16), "w_fox_o": w_fox_o.astype(BF16), "w_mem_o": w_mem_o.astype(BF16),
        "w_out": w_out.astype(BF16),
    }

    mem_kv = _matmul(mem_prompt.reshape(bp * n_mem, d),
                     jnp.concatenate([w_mem_k, w_mem_v], axis=1).astype(BF16))
    mem_k_p = mem_kv[:, :MQ].reshape(bp, n_mem, MQ)
    mem_v_p = mem_kv[:, MQ:].reshape(bp, n_mem, MQ)

    def attend_prompt(fq, fkt, fvt, lft):
        return _fox_prefill(fq, fkt, fvt, _cumsum_lanes(lft))

    y_p, fkt_p, fvt_p, lft_p, ret_p = _layer(
        x_prompt.reshape(bp * lp, d), jnp.arange(lp, dtype=F32), bp, jnp.zeros((bp, H_R, DK_R, DV_R), F32),
        mem_k_p, mem_v_p, attend_prompt, True, params)

    cache_kt = jnp.transpose(cache_fox_k, (0, 2, 3, 1))
    cache_vt = jnp.transpose(cache_fox_v, (0, 2, 3, 1))
    cache_lft = jnp.transpose(cache_fox_logf, (0, 2, 1))

    def attend_sample(fq, fk, fv, lft):
        lf_new = jnp.transpose(lft.reshape(H_F, db, ls), (1, 0, 2))
        return _fox_decode(fq, fk, fv, lf_new, cache_kt, cache_vt, cache_lft, page_table)

    y_s, fk_s, fv_s, lft_s, ret_s = _layer(
        x_sample.reshape(db * ls, d), past_len + jnp.tile(jnp.arange(ls, dtype=F32), db), 1, state_ret,
        cache_mem_k.reshape(db, n_mem, MQ), cache_mem_v.reshape(db, n_mem, MQ), attend_sample, False, params)

    to_heads = lambda t, b, l: jnp.transpose(t.reshape(b, H_F, D_F, l), (0, 3, 1, 2))
    return (y_p.reshape(bp, lp, d), y_s.reshape(db, ls, d),
            to_heads(fkt_p, bp, lp), to_heads(fvt_p, bp, lp), jnp.transpose(lft_p, (0, 2, 1)),
            ret_p, mem_k_p.reshape(bp, n_mem, H_M, D_M), mem_v_p.reshape(bp, n_mem, H_M, D_M),
            fk_s.reshape(db, ls, H_F, D_F), fv_s.reshape(db, ls, H_F, D_F),
            jnp.transpose(lft_s.reshape(H_F, db, ls), (1, 2, 0)), ret_s)
```

```python
import functools
import math

import jax
import jax.numpy as jnp
import numpy as np
from jax import lax
from jax.experimental import pallas as pl
from jax.experimental.pallas import tpu as pltpu

F32 = jnp.float32
BF16 = jnp.bfloat16

DEPTH = 1
H_R, DK_R, DV_R = 4, 64, 128
RET_CHUNK = 128
ROPE_BASE = 10000.0
H_F, D_F = 8, 64
H_M, D_M = 4, 128
N_BRANCH = 3
LN_EPS = 1e-5
GN_EPS = 1e-5
ALPHA = (2.0 * DEPTH) ** 0.25
RQ, RV, FQ, MQ = H_R * DK_R, H_R * DV_R, H_F * D_F, H_M * D_M

LANES = 128
BF16_ROWS = 16
TOKEN_TILE = 256
ATTN_TILE = 512
PREFILL_HEADS = 4
PAGES_PER_STEP = 16
DECODE_GROUPS = 2
NEG_BIG = -1e30
LOG2E = math.log2(math.e)
BIAS_PIECES = 3

_NT = (((1,), (1,)), ((), ()))
_TN = (((0,), (0,)), ((), ()))


def _resident(shape):
    return pl.BlockSpec(shape, lambda *_: (0,) * len(shape), pipeline_mode=pl.Buffered(1))


def _layer_norm(y, g, b):
    mu = jnp.mean(y, axis=-1, keepdims=True)
    yc = y - mu
    var = jnp.mean(yc * yc, axis=-1, keepdims=True)
    return yc * lax.rsqrt(var + LN_EPS) * g + b


def _silu(x):
    return x * jax.nn.sigmoid(x)


def _ffn_ln_body(x_ref, wg_ref, wu_ref, wd_ref, g_ref, b_ref, o_ref):
    x = x_ref[...]
    xb = x.astype(BF16)
    gate = jnp.dot(xb, wg_ref[...], preferred_element_type=F32)
    up = jnp.dot(xb, wu_ref[...], preferred_element_type=F32)
    act = (_silu(gate) * up).astype(BF16)
    f = jnp.dot(act, wd_ref[...], preferred_element_type=F32)
    o_ref[...] = _layer_norm(ALPHA * x + 0.5 * f, g_ref[...], b_ref[...])


def _ffn_ln(x, wg, wu, wd, g, b):
    n, d = x.shape
    d_ff = wg.shape[1]
    tm = min(TOKEN_TILE, n)
    return pl.pallas_call(
        _ffn_ln_body,
        grid=(n // tm,),
        in_specs=[pl.BlockSpec((tm, d), lambda i: (i, 0)),
                  _resident((d, d_ff)), _resident((d, d_ff)), _resident((d_ff, d)),
                  _resident((1, d)), _resident((1, d))],
        out_specs=pl.BlockSpec((tm, d), lambda i: (i, 0)),
        out_shape=jax.ShapeDtypeStruct((n, d), F32),
        name="ffn_ln",
    )(x, wg, wu, wd, g, b)


def _rotary(x, cos, sin_signed):
    width = x.shape[-1]
    half = DK_R // 2
    lane = lax.broadcasted_iota(jnp.int32, x.shape, 1)
    other = jnp.where((lane % DK_R) < half,
                      pltpu.roll(x, width - half, 1),
                      pltpu.roll(x, half, 1))
    return x * cos + other * sin_signed


def _log_sigmoid(x):
    return jnp.minimum(x, 0.0) - jnp.log1p(jnp.exp(-jnp.abs(x)))


def _in_proj_body(h_ref, cos_ref, sin_ref, w_rq, w_rk, w_rv, w_rg, w_mq, w_gl, w_fq, w_fk, w_fv, w_fl, bf_ref,
                  *rest, prefill):
    if prefill:
        w_fl_nat, bf_row_ref = rest[:2]
        rest = rest[2:]
    rq_o, rk_o, rv_o, rg_o, mq_o, gl_o = rest[:6]
    fox_o = rest[6:]
    hb = h_ref[...].astype(BF16)

    def proj(w):
        return jnp.dot(hb, w, preferred_element_type=F32)

    def proj_t(wt):
        return lax.dot_general(wt, hb, _NT, preferred_element_type=F32)

    cos, sin = cos_ref[...], sin_ref[...]
    rq_o[...] = _rotary(proj(w_rq[...]), cos, sin)
    rk_o[...] = _rotary(proj(w_rk[...]), cos, sin) * (DK_R ** -0.5)
    rv_o[...] = proj(w_rv[...]).astype(rv_o.dtype)
    rg_o[...] = proj(w_rg[...])
    mq_o[...] = proj(w_mq[...]).astype(mq_o.dtype)
    gl_o[...] = proj(w_gl[...])
    if prefill:
        fqt_o, fk_o, fkt_o, fvt_o, fvt16_o, lft_o, lf_o = fox_o
        fqt_o[0] = (proj_t(w_fq[...]) * (D_F ** -0.5 * LOG2E)).astype(BF16)
        fk_o[...] = lax.dot_general(hb, w_fk[...], _NT, preferred_element_type=F32).astype(BF16)
        fkt_o[0] = proj_t(w_fk[...])
        fvt = proj_t(w_fv[...])
        fvt_o[0] = fvt
        fvt16_o[0] = fvt.astype(BF16)
        lft_o[0] = _log_sigmoid(proj_t(w_fl[...]) + bf_ref[...])
        lf_o[...] = _log_sigmoid(proj(w_fl_nat[...]) + bf_row_ref[...])
    else:
        fq_o, fk_o, fv_o, lft_o = fox_o
        fq_o[...] = (proj(w_fq[...]) * (D_F ** -0.5)).astype(fq_o.dtype)
        fk_o[...] = proj(w_fk[...])
        fv_o[...] = proj(w_fv[...])
        lft_o[0] = _log_sigmoid(proj_t(w_fl[...]) + bf_ref[...])


def _in_proj(h, cos, sin, w, b_forget, groups, prefill, act_dtype):
    n, d = h.shape
    lg = n // groups
    tm = min(TOKEN_TILE, lg)
    nt = lg // tm
    row = lambda c: pl.BlockSpec((tm, c), lambda g, i: (g * nt + i, 0))
    tab = pl.BlockSpec((tm, RQ), lambda g, i: (i, 0))
    colt = lambda c: pl.BlockSpec((1, c, tm), lambda g, i: (g, 0, i))
    rows = lambda c, dt: jax.ShapeDtypeStruct((n, c), dt)
    cols = lambda c, dt: jax.ShapeDtypeStruct((groups, c, lg), dt)
    if prefill:
        fox_specs = [colt(FQ), row(FQ), colt(FQ), colt(FQ), colt(FQ), colt(H_F), row(H_F)]
        fox_shapes = [cols(FQ, BF16), rows(FQ, BF16), cols(FQ, F32), cols(FQ, F32), cols(FQ, BF16),
                      cols(H_F, F32), rows(H_F, F32)]
    else:
        fox_specs = [row(FQ), row(FQ), row(FQ), colt(H_F)]
        fox_shapes = [rows(FQ, act_dtype), rows(FQ, F32), rows(FQ, F32), cols(H_F, F32)]
    gates = w["gl"].shape[1]
    weights = [w[k] for k in ("rq", "rk", "rv", "rg", "mq", "gl", "fq", "fk", "fv", "fl")] + [b_forget]
    if prefill:
        weights += [w["fl_nat"], b_forget.reshape(1, H_F)]
    return pl.pallas_call(
        functools.partial(_in_proj_body, prefill=prefill),
        grid=(groups, nt),
        in_specs=[row(d), tab, tab] + [_resident(a.shape) for a in weights],
        out_specs=[row(RQ), row(RQ), row(RV), row(RV), row(MQ), row(gates)] + fox_specs,
        out_shape=[rows(RQ, F32), rows(RQ, F32), rows(RV, act_dtype), rows(RV, F32), rows(MQ, act_dtype),
                   rows(gates, F32)] + fox_shapes,
        name="in_proj",
    )(h, cos, sin, *weights)


def _retention_body(rq_ref, rk_ref, rv_ref, rg_ref, s0_ref, dmask_ref, qdec_ref, kdec_ref, gn_ref,
                    o_ref, sfin_ref, s_scr, *, chunk_decay):
    c = pl.program_id(1)

    @pl.when(c == 0)
    def _():
        s_scr[...] = s0_ref[0]

    q = rq_ref[...]
    k = rk_ref[...]
    qd = q * qdec_ref[...]
    kd = k * kdec_ref[...]
    outs = []
    for h in range(H_R):
        ks = slice(h * DK_R, (h + 1) * DK_R)
        vs = slice(h * DV_R, (h + 1) * DV_R)
        vh = rv_ref[:, vs].astype(BF16)
        a = lax.dot_general(q[:, ks].astype(BF16), k[:, ks].astype(BF16), _NT,
                            preferred_element_type=F32) * dmask_ref[h]
        s_h = s_scr[h]
        o = (jnp.dot(a.astype(BF16), vh, preferred_element_type=F32)
             + jnp.dot(qd[:, ks].astype(BF16), s_h.astype(BF16), preferred_element_type=F32))
        s_scr[h] = s_h * chunk_decay[h] + lax.dot_general(kd[:, ks].astype(BF16), vh, _TN,
                                                          preferred_element_type=F32)
        mu = jnp.mean(o, axis=-1, keepdims=True)
        oc = o - mu
        var = jnp.mean(oc * oc, axis=-1, keepdims=True)
        outs.append(_silu(rg_ref[:, vs]) * (oc * lax.rsqrt(var + GN_EPS) * gn_ref[:, vs]))
    o_ref[...] = jnp.concatenate(outs, axis=-1).astype(o_ref.dtype)

    @pl.when(c == pl.num_programs(1) - 1)
    def _():
        sfin_ref[0] = s_scr[...]


def _retention_tables(chunk):
    log_g = jnp.log1p(-jnp.exp2(-5.0 - jnp.arange(H_R, dtype=F32)))
    i = jnp.arange(chunk, dtype=F32)
    diff = i[:, None] - i[None, :]
    dmask = jnp.where(diff[None] >= 0, jnp.exp(jnp.maximum(diff, 0.0)[None] * log_g[:, None, None]), 0.0)
    q_dec = jnp.exp((i + 1.0)[:, None] * log_g[None, :])
    k_dec = jnp.exp((chunk - 1.0 - i)[:, None] * log_g[None, :])
    widen = lambda t: jnp.repeat(t, DK_R, axis=1)
    return dmask, widen(q_dec), widen(k_dec)


def _chunk_decay(chunk):
    log_g = np.log1p(-np.exp2(-5.0 - np.arange(H_R, dtype=np.float32))).astype(np.float32)
    return tuple(float(v) for v in np.exp(np.float32(chunk) * log_g).astype(np.float32))


def _retention(rq, rk, rv, rg, s0, gn_g):
    batch = s0.shape[0]
    n = rq.shape[0]
    length = n // batch
    chunk = math.gcd(length, RET_CHUNK)
    nc = length // chunk
    dmask, q_dec, k_dec = _retention_tables(chunk)
    row = lambda c: pl.BlockSpec((chunk, c), lambda b, i: (b * nc + i, 0))
    state = pl.BlockSpec((1, H_R, DK_R, DV_R), lambda b, i: (b, 0, 0, 0))
    return pl.pallas_call(
        functools.partial(_retention_body, chunk_decay=_chunk_decay(chunk)),
        grid=(batch, nc),
        in_specs=[row(RQ), row(RQ), row(RV), row(RV), state,
                  _resident((H_R, chunk, chunk)), _resident((chunk, RQ)), _resident((chunk, RQ)),
                  _resident((1, RV))],
        out_specs=[row(RV), state],
        out_shape=[jax.ShapeDtypeStruct((n, RV), rv.dtype),
                   jax.ShapeDtypeStruct((batch, H_R, DK_R, DV_R), F32)],
        scratch_shapes=[pltpu.VMEM((H_R, DK_R, DV_R), F32)],
        name="retention",
    )(rq, rk, rv, rg, s0, dmask, q_dec, k_dec, gn_g)


def _lane_suffix_sum(x):
    lane = lax.broadcasted_iota(jnp.int32, x.shape, 1)
    d = 1
    while d < LANES:
        x = x + jnp.where(lane < LANES - d, pltpu.roll(x, LANES - d, 1), 0.0)
        d *= 2
    return x


def _forget_bias_body(lf_ref, tri_ref, place_ref, o_ref, carry_scr):
    @pl.when(pl.program_id(1) == 0)
    def _():
        carry_scr[...] = jnp.zeros(carry_scr.shape, F32)

    rows = lf_ref.shape[0]
    c = jnp.dot(tri_ref[...], lf_ref[...], precision=lax.Precision.HIGHEST,
                preferred_element_type=F32) + carry_scr[...]
    carry_scr[...] = c[rows - 1:rows, :]
    bias = c * (-LOG2E)
    hi = bias.astype(BF16)
    rest = bias - hi.astype(F32)
    mid = rest.astype(BF16)
    lo = (rest - mid.astype(F32)).astype(BF16)
    for j in range(H_F // 2):
        plane = sum(jnp.dot(piece, place_ref[i, j], preferred_element_type=F32)
                    for i, piece in enumerate((hi, mid, lo)))
        o_ref[0, j] = plane.astype(BF16)


def _bias_placement():
    place = np.zeros((BIAS_PIECES, H_F // 2, H_F, 2 * D_F), np.float32)
    for i in range(BIAS_PIECES):
        for j in range(H_F // 2):
            place[i, j, 2 * j, D_F + i] = 1.0
            place[i, j, 2 * j + 1, i] = 1.0
    return jnp.asarray(place, BF16)


def _forget_bias(lf, batch):
    length = lf.shape[0] // batch
    rows = min(ATTN_TILE, length)
    nt = length // rows
    tri = jnp.asarray(np.tril(np.ones((rows, rows), np.float32)))
    place = _bias_placement()
    return pl.pallas_call(
        _forget_bias_body, grid=(batch, nt),
        in_specs=[pl.BlockSpec((rows, H_F), lambda b, i: (b * nt + i, 0)), _resident(tri.shape),
                  _resident(place.shape)],
        out_specs=pl.BlockSpec((1, H_F // 2, rows, 2 * D_F), lambda b, i: (b, 0, i, 0)),
        out_shape=jax.ShapeDtypeStruct((batch, H_F // 2, length, 2 * D_F), BF16),
        scratch_shapes=[pltpu.VMEM((1, H_F), F32)],
        name="forget_bias",
    )(lf, tri, place)


def _fox_prefill_body(qi_tab, ki_tab, qt_ref, k_ref, kb_ref, vt_ref, o_ref, qa_scr, m_scr, l_scr, acc_scr, *,
                      heads):
    step_id = pl.program_id(2)
    qi = qi_tab[step_id]
    ki = ki_tab[step_id]
    pair_w = 2 * D_F

    @pl.when(ki == 0)
    def _():
        m_scr[...] = jnp.full(m_scr.shape, NEG_BIG, F32)
        l_scr[...] = jnp.zeros(l_scr.shape, F32)
        acc_scr[...] = jnp.zeros(acc_scr.shape, F32)
        row = lax.broadcasted_iota(jnp.int32, (pair_w, qt_ref.shape[2]), 0)
        for h in range(heads):
            pair = qt_ref[0, (h // 2) * pair_w:(h // 2 + 1) * pair_w, :]
            first = D_F if h % 2 == 0 else 0
            qa_scr[h] = jnp.where((row >= first) & (row < first + BIAS_PIECES), jnp.ones_like(pair), pair)

    def step(masked):
        lane = lax.broadcasted_iota(jnp.int32, (k_ref.shape[0], pair_w), 1)
        for h in range(heads):
            j = h // 2
            own = (lane < D_F) if h % 2 == 0 else (lane >= D_F)
            ka = jnp.where(own, k_ref[:, j * pair_w:(j + 1) * pair_w], kb_ref[0, j])
            st = jnp.dot(ka, qa_scr[h], preferred_element_type=F32)
            if masked:
                key = lax.broadcasted_iota(jnp.int32, st.shape, 0)
                qry = lax.broadcasted_iota(jnp.int32, st.shape, 1)
                st = jnp.where(key <= qry, st, NEG_BIG)
            m_old = m_scr[h]
            m_new = jnp.maximum(m_old, jnp.max(st, axis=0, keepdims=True))
            alpha = jnp.exp2(m_old - m_new)
            pt = jnp.exp2(st - m_new)
            l_scr[h] = alpha * l_scr[h] + jnp.sum(pt, axis=0, keepdims=True)
            rows = slice(h * D_F, (h + 1) * D_F)
            acc_scr[rows, :] = alpha * acc_scr[rows, :] + jnp.dot(vt_ref[0, rows, :], pt.astype(BF16),
                                                                  preferred_element_type=F32)
            m_scr[h] = m_new

    @pl.when(ki < qi)
    def _():
        step(False)

    @pl.when(ki == qi)
    def _():
        step(True)
        out_t = jnp.concatenate([acc_scr[h * D_F:(h + 1) * D_F, :] / l_scr[h] for h in range(heads)], axis=0)
        o_ref[...] = out_t.T.astype(BF16)


def _fox_prefill(fqt, fk, kbias, fvt):
    batch, _, length = fqt.shape
    tile = min(ATTN_TILE, length)
    nt = length // tile
    heads = PREFILL_HEADS
    width = heads * D_F
    pairs = [(q, k) for q in range(nt) for k in range(q + 1)]
    qi_tab = jnp.asarray([q for q, _ in pairs], jnp.int32)
    ki_tab = jnp.asarray([k for _, k in pairs], jnp.int32)
    grid_spec = pltpu.PrefetchScalarGridSpec(
        num_scalar_prefetch=2,
        grid=(batch, H_F // heads, len(pairs)),
        in_specs=[pl.BlockSpec((1, width, tile), lambda b, g, p, qt, kt: (b, g, qt[p])),
                  pl.BlockSpec((tile, width), lambda b, g, p, qt, kt: (b * nt + kt[p], g)),
                  pl.BlockSpec((1, heads // 2, tile, 2 * D_F), lambda b, g, p, qt, kt: (b, g, kt[p], 0)),
                  pl.BlockSpec((1, width, tile), lambda b, g, p, qt, kt: (b, g, kt[p]))],
        out_specs=pl.BlockSpec((tile, width), lambda b, g, p, qt, kt: (b * nt + qt[p], g)),
        scratch_shapes=[pltpu.VMEM((heads, 2 * D_F, tile), BF16), pltpu.VMEM((heads, 1, tile), F32),
                        pltpu.VMEM((heads, 1, tile), F32), pltpu.VMEM((width, tile), F32)],
    )
    return pl.pallas_call(
        functools.partial(_fox_prefill_body, heads=heads),
        grid_spec=grid_spec,
        out_shape=jax.ShapeDtypeStruct(fk.shape, BF16),
        name="fox_prefill",
    )(qi_tab, ki_tab, fqt, fk, kbias, fvt)


def _fox_decode_body(pt_ref, q_ref, kn_ref, vn_ref, lfn_ref, k_hbm, v_hbm, lf_hbm, o_ref,
                     k_buf, v_buf, lf_buf, sems, qbd_scr, m_scr, l_scr, acc_scr, r_scr, *, pages, t_new):
    b_idx = pl.program_id(0)
    p_idx = pl.program_id(1)
    steps = pl.num_programs(1)
    n_pages = steps * pages
    rows = H_F * t_new
    half = pages // DECODE_GROUPS
    step_id = b_idx * steps + p_idx
    slot = step_id % 2

    def page_copies(b, p, slot):
        out = []
        for j in range(pages):
            page = pt_ref[b, n_pages - 1 - (p * pages + j)]
            out.append(pltpu.make_async_copy(k_hbm.at[page], k_buf.at[slot, j], sems.at[0, slot]))
            out.append(pltpu.make_async_copy(v_hbm.at[page], v_buf.at[slot, j], sems.at[1, slot]))
            out.append(pltpu.make_async_copy(lf_hbm.at[page], lf_buf.at[slot, j], sems.at[2, slot]))
        return out

    @pl.when(step_id == 0)
    def _():
        for c in page_copies(b_idx, p_idx, slot):
            c.start()

    @pl.when(step_id + 1 < pl.num_programs(0) * steps)
    def _():
        wrap = p_idx + 1 == steps
        for c in page_copies(jnp.where(wrap, b_idx + 1, b_idx), jnp.where(wrap, 0, p_idx + 1), 1 - slot):
            c.start()

    def expand(x):
        return jnp.broadcast_to(x[:, None, :], (H_F, t_new, x.shape[-1])).reshape(rows, x.shape[-1])

    @pl.when(p_idx == 0)
    def _():
        q = jnp.concatenate([q_ref[...].astype(F32)] * H_F, axis=0)
        r_head = lax.broadcasted_iota(jnp.int32, (rows, FQ), 0) // t_new
        l_head = lax.broadcasted_iota(jnp.int32, (rows, FQ), 1) // D_F
        qbd = jnp.where(r_head == l_head, q, 0.0).astype(BF16)
        qbd_scr[...] = qbd
        lfn = lfn_ref[0]
        lane = lax.broadcasted_iota(jnp.int32, lfn.shape, 1)
        cnew = jnp.zeros_like(lfn)
        for i in range(t_new):
            cnew = cnew + jnp.where(lane >= i, lfn[:, i:i + 1], 0.0)
        s = lax.dot_general(qbd, kn_ref[...].astype(BF16), _NT, preferred_element_type=F32) - expand(cnew)
        t_q = lax.broadcasted_iota(jnp.int32, s.shape, 0) % t_new
        t_k = lax.broadcasted_iota(jnp.int32, s.shape, 1)
        s = jnp.where(t_k <= t_q, s, NEG_BIG)
        m = jnp.max(s, axis=-1, keepdims=True)
        p = jnp.exp(s - m)
        m_scr[...] = jnp.full(m_scr.shape, NEG_BIG, F32)
        l_scr[...] = jnp.zeros(l_scr.shape, F32)
        acc_scr[...] = jnp.zeros(acc_scr.shape, F32)
        m_scr[0] = m
        l_scr[0] = jnp.sum(p, axis=-1, keepdims=True)
        acc_scr[0] = jnp.dot(p.astype(BF16), vn_ref[...].astype(BF16), preferred_element_type=F32)
        r_scr[...] = jnp.zeros(r_scr.shape, F32)

    for c in page_copies(b_idx, p_idx, slot):
        c.wait()

    qbd = qbd_scr[...]
    run = r_scr[...]
    biases = []
    for j in range(pages):
        lf = lf_buf[slot, j]
        suffix = _lane_suffix_sum(lf)
        biases.append(expand(suffix - lf + run))
        run = run + suffix[:, 0:1]
    r_scr[...] = run
    for g in range(DECODE_GROUPS):
        js = range(g * half, (g + 1) * half)
        s = jnp.concatenate(
            [jnp.dot(qbd, k_buf[slot, j].reshape(FQ, LANES).astype(BF16), preferred_element_type=F32)
             + biases[j] for j in js], axis=-1)
        m_old = m_scr[g]
        m_new = jnp.maximum(m_old, jnp.max(s, axis=-1, keepdims=True))
        alpha = jnp.exp(m_old - m_new)
        p = jnp.exp(s - m_new)
        l_scr[g] = alpha * l_scr[g] + jnp.sum(p, axis=-1, keepdims=True)
        p = p.astype(BF16)
        acc = alpha * acc_scr[g]
        for i, j in enumerate(js):
            acc = acc + lax.dot_general(p[:, i * LANES:(i + 1) * LANES],
                                        v_buf[slot, j].reshape(FQ, LANES).astype(BF16), _NT,
                                        preferred_element_type=F32)
        acc_scr[g] = acc
        m_scr[g] = m_new

    @pl.when(p_idx == steps - 1)
    def _():
        m = jnp.max(m_scr[...], axis=0)
        scale = jnp.exp(m_scr[...] - m)
        full = jnp.sum(scale * acc_scr[...], axis=0) / jnp.sum(scale * l_scr[...], axis=0)
        l_head = lax.broadcasted_iota(jnp.int32, (t_new, FQ), 1) // D_F
        out = jnp.zeros((t_new, FQ), F32)
        for h in range(H_F):
            out = out + jnp.where(l_head == h, full[h * t_new:(h + 1) * t_new, :], 0.0)
        o_ref[...] = out.astype(o_ref.dtype)


def _fox_decode(fq, fk, fv, lf_new, cache_kt, cache_vt, cache_lft, page_table):
    db, n_pages = page_table.shape
    t_new = fq.shape[0] // db
    pages = math.gcd(PAGES_PER_STEP, n_pages)
    assert pages % DECODE_GROUPS == 0
    rows = H_F * t_new
    page_len = cache_kt.shape[-1]
    assert page_len == LANES
    tok = pl.BlockSpec((t_new, FQ), lambda b, p, pt: (b, 0))
    hbm = pl.BlockSpec(memory_space=pltpu.HBM)
    grid_spec = pltpu.PrefetchScalarGridSpec(
        num_scalar_prefetch=1,
        grid=(db, n_pages // pages),
        in_specs=[tok, tok, tok, pl.BlockSpec((1, H_F, t_new), lambda b, p, pt: (b, 0, 0)), hbm, hbm, hbm],
        out_specs=tok,
        scratch_shapes=[pltpu.VMEM((2, pages, H_F, D_F, page_len), F32),
                        pltpu.VMEM((2, pages, H_F, D_F, page_len), F32),
                        pltpu.VMEM((2, pages, H_F, page_len), F32),
                        pltpu.SemaphoreType.DMA((3, 2)),
                        pltpu.VMEM((rows, FQ), BF16),
                        pltpu.VMEM((DECODE_GROUPS, rows, 1), F32), pltpu.VMEM((DECODE_GROUPS, rows, 1), F32),
                        pltpu.VMEM((DECODE_GROUPS, rows, FQ), F32), pltpu.VMEM((H_F, 1), F32)],
    )
    return pl.pallas_call(
        functools.partial(_fox_decode_body, pages=pages, t_new=t_new),
        grid_spec=grid_spec,
        out_shape=jax.ShapeDtypeStruct(fq.shape, fq.dtype),
        compiler_params=pltpu.CompilerParams(dimension_semantics=("arbitrary", "arbitrary")),
        name="fox_decode",
    )(page_table, fq, fk, fv, lf_new, cache_kt, cache_vt, cache_lft)


def _mem_attn_body(q_ref, k_ref, v_ref, o_ref):
    outs = []
    for h in range(H_M):
        sl = slice(h * D_M, (h + 1) * D_M)
        s = lax.dot_general(q_ref[:, sl].astype(BF16), k_ref[0, :, sl].astype(BF16), _NT,
                            preferred_element_type=F32) * (D_M ** -0.5)
        p = jnp.exp(s - jnp.max(s, axis=-1, keepdims=True))
        l = jnp.sum(p, axis=-1, keepdims=True)
        outs.append(jnp.dot(p.astype(BF16), v_ref[0, :, sl].astype(BF16), preferred_element_type=F32) / l)
    o_ref[...] = jnp.concatenate(outs, axis=-1).astype(o_ref.dtype)


def _mem_attn(mq, mem_k, mem_v):
    batch, n_mem, _ = mem_k.shape
    length = mq.shape[0] // batch
    tq = min(ATTN_TILE, length)
    nt = length // tq
    row = pl.BlockSpec((tq, MQ), lambda b, i: (b * nt + i, 0))
    mem = pl.BlockSpec((1, n_mem, MQ), lambda b, i: (b, 0, 0))
    return pl.pallas_call(_mem_attn_body, grid=(batch, nt), in_specs=[row, mem, mem], out_specs=row,
                          out_shape=jax.ShapeDtypeStruct(mq.shape, mq.dtype), name="mem_attn")(mq, mem_k, mem_v)


def _matmul_body(x_ref, w_ref, o_ref):
    o_ref[...] = jnp.dot(x_ref[...].astype(BF16), w_ref[...], preferred_element_type=F32)


def _matmul(x, w):
    n, d = x.shape
    tm = min(TOKEN_TILE, n)
    return pl.pallas_call(
        _matmul_body, grid=(n // tm,),
        in_specs=[pl.BlockSpec((tm, d), lambda i: (i, 0)), _resident(w.shape)],
        out_specs=pl.BlockSpec((tm, w.shape[1]), lambda i: (i, 0)),
        out_shape=jax.ShapeDtypeStruct((n, w.shape[1]), F32), name="mem_kv_proj")(x, w)


def _merge_body(h_ref, ret_ref, fox_ref, mem_ref, gl_ref, w_ret, w_fox, w_mem, w_out, bg_ref, g_ref, b_ref,
                o_ref):
    d = h_ref.shape[1]
    mix = jnp.zeros(h_ref.shape, F32)
    for i, (x_ref, w_ref) in enumerate(((ret_ref, w_ret), (fox_ref, w_fox), (mem_ref, w_mem))):
        gate = jax.nn.sigmoid(gl_ref[:, i * d:(i + 1) * d] + bg_ref[i:i + 1, :])
        mix = mix + gate * jnp.dot(x_ref[...].astype(BF16), w_ref[...], preferred_element_type=F32)
    mixed = jnp.dot(mix.astype(BF16), w_out[...], preferred_element_type=F32)
    o_ref[...] = _layer_norm(ALPHA * h_ref[...] + mixed, g_ref[...], b_ref[...])


def _merge(h, ret, fox, mem, gl, w_ret, w_fox, w_mem, w_out, b_gate, g, b):
    n, d = h.shape
    tm = min(TOKEN_TILE, n)
    row = lambda c: pl.BlockSpec((tm, c), lambda i: (i, 0))
    return pl.pallas_call(
        _merge_body, grid=(n // tm,),
        in_specs=[row(d), row(RV), row(FQ), row(MQ), row(N_BRANCH * d),
                  _resident(w_ret.shape), _resident(w_fox.shape), _resident(w_mem.shape), _resident(w_out.shape),
                  _resident(b_gate.shape), _resident((1, d)), _resident((1, d))],
        out_specs=row(d),
        out_shape=jax.ShapeDtypeStruct((n, d), F32), name="merge",
    )(h, ret, fox, mem, gl, w_ret, w_fox, w_mem, w_out, b_gate, g, b)


def _rope_tables(pos):
    half = DK_R // 2
    inv = ROPE_BASE ** (-jnp.arange(half, dtype=F32) / half)
    ang = pos[:, None] * inv[None, :]
    cos, sin = jnp.cos(ang), jnp.sin(ang)
    return (jnp.tile(jnp.concatenate([cos, cos], axis=1), (1, H_R)),
            jnp.tile(jnp.concatenate([-sin, sin], axis=1), (1, H_R)))


def _split_w_in(w_in, d_model, prefill):
    sizes = (RQ, RQ, RV, RV, FQ, FQ, FQ, H_F, MQ, N_BRANCH * d_model)
    names = ("rq", "rk", "rv", "rg", "fq", "fk", "fv", "fl", "mq", "gl")
    offs = np.concatenate([[0], np.cumsum(sizes)])
    w = {nm: w_in[:, int(offs[i]):int(offs[i + 1])].astype(BF16) for i, nm in enumerate(names)}
    if prefill:
        w["fl_nat"] = w["fl"]
        w["fq"], w["fk"], w["fv"] = w["fq"].T, w["fk"].T, w["fv"].T
    w["fl"] = w["fl"].T
    return w


def _layer(x, pos, groups, s0, mem_k, mem_v, fox_attend, prefill, p):
    d = x.shape[1]
    ln_g, ln_b = p["ln_g"], p["ln_b"]
    act_dtype = BF16 if (x.shape[0] // s0.shape[0]) % BF16_ROWS == 0 else F32
    h = _ffn_ln(x, p["wg"][0], p["wu"][0], p["wd"][0], ln_g[0:1], ln_b[0:1])
    cos, sin = _rope_tables(pos)
    rq, rk, rv, rg, mq, gl, *fox_in = _in_proj(
        h, cos, sin, _split_w_in(p["w_in"], d, prefill), p["b_forget"], groups, prefill, act_dtype)
    ret, ret_state = _retention(rq, rk, rv, rg, s0, p["ret_gn_g"])
    fox = fox_attend(*fox_in)
    mem = _mem_attn(mq, mem_k, mem_v)
    h = _merge(h, ret, fox, mem, gl, p["w_ret_o"], p["w_fox_o"], p["w_mem_o"], p["w_out"], p["b_gate"],
               ln_g[1:2], ln_b[1:2])
    y = _ffn_ln(h, p["wg"][1], p["wu"][1], p["wd"][1], ln_g[2:3], ln_b[2:3])
    return y, fox_in, ret_state


def kernel(x_prompt, x_sample, mem_prompt, cache_fox_k, cache_fox_v, cache_fox_logf, state_ret, cache_mem_k, cache_mem_v, page_table, ln_g, ln_b, w_ffn_gate, w_ffn_up, w_ffn_down, w_in, b_forget, b_gate, ret_gn_g, w_ret_o, w_fox_o, w_mem_k, w_mem_v, w_mem_o, w_out):
    bp, lp, d = x_prompt.shape
    db, ls, _ = x_sample.shape
    n_mem = mem_prompt.shape[1]
    n_pages = page_table.shape[1]
    page = cache_fox_k.shape[1]
    past_len = n_pages * page
    params = {
        "ln_g": ln_g, "ln_b": ln_b,
        "wg": w_ffn_gate.astype(BF16), "wu": w_ffn_up.astype(BF16), "wd": w_ffn_down.astype(BF16),
        "w_in": w_in, "b_forget": b_forget.reshape(H_F, 1), "b_gate": b_gate,
        "ret_gn_g": ret_gn_g.reshape(1, RV),
        "w_ret_o": w_ret_o.astype(BF16), "w_fox_o": w_fox_o.astype(BF16), "w_mem_o": w_mem_o.astype(BF16),
        "w_out": w_out.astype(BF16),
    }

    mem_kv = _matmul(mem_prompt.reshape(bp * n_mem, d),
                     jnp.concatenate([w_mem_k, w_mem_v], axis=1).astype(BF16))
    mem_k_p = mem_kv[:, :MQ].reshape(bp, n_mem, MQ)
    mem_v_p = mem_kv[:, MQ:].reshape(bp, n_mem, MQ)

    def attend_prompt(fqt, fk, fkt, fvt, fvt16, lft, lf):
        return _fox_prefill(fqt, fk, _forget_bias(lf, bp), fvt16)

    y_p, (_, _, fkt_p, fvt_p, _, lft_p, _), ret_p = _layer(
        x_prompt.reshape(bp * lp, d), jnp.arange(lp, dtype=F32), bp, jnp.zeros((bp, H_R, DK_R, DV_R), F32),
        mem_k_p, mem_v_p, attend_prompt, True, params)

    cache_kt = jnp.transpose(cache_fox_k, (0, 2, 3, 1))
    cache_vt = jnp.transpose(cache_fox_v, (0, 2, 3, 1))
    cache_lft = jnp.transpose(cache_fox_logf, (0, 2, 1))

    def attend_sample(fq, fk, fv, lft):
        lf_new = jnp.transpose(lft.reshape(H_F, db, ls), (1, 0, 2))
        return _fox_decode(fq, fk, fv, lf_new, cache_kt, cache_vt, cache_lft, page_table)

    y_s, (_, fk_s, fv_s, lft_s), ret_s = _layer(
        x_sample.reshape(db * ls, d), past_len + jnp.tile(jnp.arange(ls, dtype=F32), db), 1, state_ret,
        cache_mem_k.reshape(db, n_mem, MQ), cache_mem_v.reshape(db, n_mem, MQ), attend_sample, False, params)

    to_heads = lambda t, b, l: jnp.transpose(t.reshape(b, H_F, D_F, l), (0, 3, 1, 2))
    return (y_p.reshape(bp, lp, d), y_s.reshape(db, ls, d),
            to_heads(fkt_p, bp, lp), to_heads(fvt_p, bp, lp), jnp.transpose(lft_p, (0, 2, 1)),
            ret_p, mem_k_p.reshape(bp, n_mem, H_M, D_M), mem_v_p.reshape(bp, n_mem, H_M, D_M),
            fk_s.reshape(db, ls, H_F, D_F), fv_s.reshape(db, ls, H_F, D_F),
            jnp.transpose(lft_s.reshape(H_F, db, ls), (1, 2, 0)), ret_s)
```

```python
import functools
import math

import jax
import jax.numpy as jnp
import numpy as np
from jax import lax
from jax.experimental import pallas as pl
from jax.experimental.pallas import tpu as pltpu

F32 = jnp.float32
BF16 = jnp.bfloat16

DEPTH = 1
H_R, DK_R, DV_R = 4, 64, 128
RET_CHUNK = 128
RET_CHUNKS_PER_STEP = 4
ROPE_BASE = 10000.0
H_F, D_F = 8, 64
H_M, D_M = 4, 128
N_BRANCH = 3
LN_EPS = 1e-5
GN_EPS = 1e-5
ALPHA = (2.0 * DEPTH) ** 0.25
RQ, RV, FQ, MQ = H_R * DK_R, H_R * DV_R, H_F * D_F, H_M * D_M

LANES = 128
BF16_ROWS = 16
TOKEN_TILE = 256
WIDE_TOKEN_TILE = 512
ATTN_TILE = 512
PREFILL_HEADS = 8
PAGES_PER_STEP = 16
DECODE_GROUPS = 2
DECODE_SLOTS = 3
NEG_BIG = -1e30
LOG2E = math.log2(math.e)
BIAS_PIECES = 3

_NT = (((1,), (1,)), ((), ()))
_TN = (((0,), (0,)), ((), ()))


def _resident(shape):
    return pl.BlockSpec(shape, lambda *_: (0,) * len(shape), pipeline_mode=pl.Buffered(1))


def _layer_norm(y, g, b):
    mu = jnp.mean(y, axis=-1, keepdims=True)
    yc = y - mu
    var = jnp.mean(yc * yc, axis=-1, keepdims=True)
    return yc * lax.rsqrt(var + LN_EPS) * g + b


def _silu(x):
    return x * jax.nn.sigmoid(x)


def _ffn_ln_body(x_ref, wg_ref, wu_ref, wd_ref, g_ref, b_ref, o_ref):
    x = x_ref[...]
    xb = x.astype(BF16)
    gate = jnp.dot(xb, wg_ref[...], preferred_element_type=F32)
    up = jnp.dot(xb, wu_ref[...], preferred_element_type=F32)
    act = (_silu(gate) * up).astype(BF16)
    f = jnp.dot(act, wd_ref[...], preferred_element_type=F32)
    o_ref[...] = _layer_norm(ALPHA * x + 0.5 * f, g_ref[...], b_ref[...])


def _ffn_ln(x, wg, wu, wd, g, b):
    n, d = x.shape
    d_ff = wg.shape[1]
    tm = min(WIDE_TOKEN_TILE, n)
    return pl.pallas_call(
        _ffn_ln_body,
        grid=(n // tm,),
        in_specs=[pl.BlockSpec((tm, d), lambda i: (i, 0)),
                  _resident((d, d_ff)), _resident((d, d_ff)), _resident((d_ff, d)),
                  _resident((1, d)), _resident((1, d))],
        out_specs=pl.BlockSpec((tm, d), lambda i: (i, 0)),
        out_shape=jax.ShapeDtypeStruct((n, d), F32),
        name="ffn_ln",
    )(x, wg, wu, wd, g, b)


def _rotary(x, cos, sin_signed):
    width = x.shape[-1]
    half = DK_R // 2
    lane = lax.broadcasted_iota(jnp.int32, x.shape, 1)
    other = jnp.where((lane % DK_R) < half,
                      pltpu.roll(x, width - half, 1),
                      pltpu.roll(x, half, 1))
    return x * cos + other * sin_signed


def _log_sigmoid(x):
    return jnp.minimum(x, 0.0) - jnp.log1p(jnp.exp(-jnp.abs(x)))


def _in_proj_body(h_ref, cos_ref, sin_ref, w_rq, w_rk, w_rv, w_rg, w_mq, w_gl, w_fq, w_fk, w_fv, w_fl, bf_ref,
                  *rest, prefill):
    if prefill:
        w_fl_nat, bf_row_ref = rest[:2]
        rest = rest[2:]
    rq_o, rk_o, rv_o, rg_o, mq_o, gl_o = rest[:6]
    fox_o = rest[6:]
    hb = h_ref[...].astype(BF16)

    def proj(w):
        return jnp.dot(hb, w, preferred_element_type=F32)

    def proj_t(wt):
        return lax.dot_general(wt, hb, _NT, preferred_element_type=F32)

    cos, sin = cos_ref[...], sin_ref[...]
    rq_o[...] = _rotary(proj(w_rq[...]), cos, sin)
    rk_o[...] = _rotary(proj(w_rk[...]), cos, sin) * (DK_R ** -0.5)
    rv_o[...] = proj(w_rv[...]).astype(rv_o.dtype)
    rg_o[...] = proj(w_rg[...])
    mq_o[...] = proj(w_mq[...]).astype(mq_o.dtype)
    gl_o[...] = proj(w_gl[...])
    if prefill:
        fqt_o, fk_o, fkt_o, fvt_o, fvt16_o, lft_o, lf_o = fox_o
        fqt_o[0] = (proj_t(w_fq[...]) * (D_F ** -0.5 * LOG2E)).astype(BF16)
        fk_o[...] = lax.dot_general(hb, w_fk[...], _NT, preferred_element_type=F32).astype(BF16)
        fkt_o[0] = proj_t(w_fk[...])
        fvt = proj_t(w_fv[...])
        fvt_o[0] = fvt
        fvt16_o[0] = fvt.astype(BF16)
        lft_o[0] = _log_sigmoid(proj_t(w_fl[...]) + bf_ref[...])
        lf_o[...] = _log_sigmoid(proj(w_fl_nat[...]) + bf_row_ref[...])
    else:
        fq_o, fk_o, fv_o, lft_o = fox_o
        fq_o[...] = (proj(w_fq[...]) * (D_F ** -0.5)).astype(fq_o.dtype)
        fk_o[...] = proj(w_fk[...])
        fv_o[...] = proj(w_fv[...])
        lft_o[0] = _log_sigmoid(proj_t(w_fl[...]) + bf_ref[...])


def _in_proj(h, cos, sin, w, b_forget, groups, prefill, act_dtype):
    n, d = h.shape
    lg = n // groups
    tm = min(TOKEN_TILE, lg)
    nt = lg // tm
    row = lambda c: pl.BlockSpec((tm, c), lambda g, i: (g * nt + i, 0))
    tab = pl.BlockSpec((tm, RQ), lambda g, i: (i, 0))
    colt = lambda c: pl.BlockSpec((1, c, tm), lambda g, i: (g, 0, i))
    rows = lambda c, dt: jax.ShapeDtypeStruct((n, c), dt)
    cols = lambda c, dt: jax.ShapeDtypeStruct((groups, c, lg), dt)
    if prefill:
        fox_specs = [colt(FQ), row(FQ), colt(FQ), colt(FQ), colt(FQ), colt(H_F), row(H_F)]
        fox_shapes = [cols(FQ, BF16), rows(FQ, BF16), cols(FQ, F32), cols(FQ, F32), cols(FQ, BF16),
                      cols(H_F, F32), rows(H_F, F32)]
    else:
        fox_specs = [row(FQ), row(FQ), row(FQ), colt(H_F)]
        fox_shapes = [rows(FQ, act_dtype), rows(FQ, F32), rows(FQ, F32), cols(H_F, F32)]
    gates = w["gl"].shape[1]
    weights = [w[k] for k in ("rq", "rk", "rv", "rg", "mq", "gl", "fq", "fk", "fv", "fl")] + [b_forget]
    if prefill:
        weights += [w["fl_nat"], b_forget.reshape(1, H_F)]
    return pl.pallas_call(
        functools.partial(_in_proj_body, prefill=prefill),
        grid=(groups, nt),
        in_specs=[row(d), tab, tab] + [_resident(a.shape) for a in weights],
        out_specs=[row(RQ), row(RQ), row(RV), row(RV), row(MQ), row(gates)] + fox_specs,
        out_shape=[rows(RQ, F32), rows(RQ, F32), rows(RV, act_dtype), rows(RV, F32), rows(MQ, act_dtype),
                   rows(gates, F32)] + fox_shapes,
        name="in_proj",
    )(h, cos, sin, *weights)


def _retention_body(rq_ref, rk_ref, rv_ref, rg_ref, s0_ref, dmask_ref, qdec_ref, kdec_ref, gn_ref,
                    o_ref, sfin_ref, s_scr, *, chunk_decay):
    c = pl.program_id(1)

    @pl.when(c == 0)
    def _():
        s_scr[...] = s0_ref[0]

    chunk = dmask_ref.shape[1]
    for h in range(H_R):
        ks = slice(h * DK_R, (h + 1) * DK_R)
        vs = slice(h * DV_R, (h + 1) * DV_R)
        s_h = s_scr[h]
        for i in range(rq_ref.shape[0] // chunk):
            rs = slice(i * chunk, (i + 1) * chunk)
            q = rq_ref[rs, ks]
            k = rk_ref[rs, ks]
            vh = rv_ref[rs, vs].astype(BF16)
            a = lax.dot_general(q.astype(BF16), k.astype(BF16), _NT,
                                preferred_element_type=F32) * dmask_ref[h]
            o = (jnp.dot(a.astype(BF16), vh, preferred_element_type=F32)
                 + jnp.dot((q * qdec_ref[:, ks]).astype(BF16), s_h.astype(BF16), preferred_element_type=F32))
            s_h = s_h * chunk_decay[h] + lax.dot_general((k * kdec_ref[:, ks]).astype(BF16), vh, _TN,
                                                         preferred_element_type=F32)
            mu = jnp.mean(o, axis=-1, keepdims=True)
            oc = o - mu
            var = jnp.mean(oc * oc, axis=-1, keepdims=True)
            o_ref[rs, vs] = (_silu(rg_ref[rs, vs]) * (oc * lax.rsqrt(var + GN_EPS) * gn_ref[:, vs])
                             ).astype(o_ref.dtype)
        s_scr[h] = s_h

    @pl.when(c == pl.num_programs(1) - 1)
    def _():
        sfin_ref[0] = s_scr[...]


def _retention_tables(chunk):
    log_g = jnp.log1p(-jnp.exp2(-5.0 - jnp.arange(H_R, dtype=F32)))
    i = jnp.arange(chunk, dtype=F32)
    diff = i[:, None] - i[None, :]
    dmask = jnp.where(diff[None] >= 0, jnp.exp(jnp.maximum(diff, 0.0)[None] * log_g[:, None, None]), 0.0)
    q_dec = jnp.exp((i + 1.0)[:, None] * log_g[None, :])
    k_dec = jnp.exp((chunk - 1.0 - i)[:, None] * log_g[None, :])
    widen = lambda t: jnp.repeat(t, DK_R, axis=1)
    return dmask, widen(q_dec), widen(k_dec)


def _chunk_decay(chunk):
    log_g = np.log1p(-np.exp2(-5.0 - np.arange(H_R, dtype=np.float32))).astype(np.float32)
    return tuple(float(v) for v in np.exp(np.float32(chunk) * log_g).astype(np.float32))


def _retention(rq, rk, rv, rg, s0, gn_g):
    batch = s0.shape[0]
    n = rq.shape[0]
    length = n // batch
    chunk = math.gcd(length, RET_CHUNK)
    rows = chunk * math.gcd(length // chunk, RET_CHUNKS_PER_STEP)
    nc = length // rows
    dmask, q_dec, k_dec = _retention_tables(chunk)
    row = lambda c: pl.BlockSpec((rows, c), lambda b, i: (b * nc + i, 0))
    state = pl.BlockSpec((1, H_R, DK_R, DV_R), lambda b, i: (b, 0, 0, 0))
    return pl.pallas_call(
        functools.partial(_retention_body, chunk_decay=_chunk_decay(chunk)),
        grid=(batch, nc),
        in_specs=[row(RQ), row(RQ), row(RV), row(RV), state,
                  _resident((H_R, chunk, chunk)), _resident((chunk, RQ)), _resident((chunk, RQ)),
                  _resident((1, RV))],
        out_specs=[row(RV), state],
        out_shape=[jax.ShapeDtypeStruct((n, RV), rv.dtype),
                   jax.ShapeDtypeStruct((batch, H_R, DK_R, DV_R), F32)],
        scratch_shapes=[pltpu.VMEM((H_R, DK_R, DV_R), F32)],
        name="retention",
    )(rq, rk, rv, rg, s0, dmask, q_dec, k_dec, gn_g)


def _lane_suffix_sum(x):
    lane = lax.broadcasted_iota(jnp.int32, x.shape, 1)
    d = 1
    while d < LANES:
        x = x + jnp.where(lane < LANES - d, pltpu.roll(x, LANES - d, 1), 0.0)
        d *= 2
    return x


def _bf16_pieces(x):
    pieces = []
    for _ in range(BIAS_PIECES):
        piece = x.astype(BF16)
        pieces.append(piece)
        x = x - piece.astype(F32)
    return pieces


def _forget_bias_body(lf_ref, tri_ref, place_ref, o_ref, carry_scr):
    @pl.when(pl.program_id(1) == 0)
    def _():
        carry_scr[...] = jnp.zeros(carry_scr.shape, F32)

    rows = lf_ref.shape[0]
    c = carry_scr[...] + sum(jnp.dot(tri_ref[...], piece, preferred_element_type=F32)
                             for piece in _bf16_pieces(lf_ref[...]))
    carry_scr[...] = c[rows - 1:rows, :]
    for j in range(H_F // 2):
        plane = sum(jnp.dot(piece, place_ref[i, j], preferred_element_type=F32)
                    for i, piece in enumerate(_bf16_pieces(c * (-LOG2E))))
        o_ref[0, j] = plane.astype(BF16)


def _bias_placement():
    place = np.zeros((BIAS_PIECES, H_F // 2, H_F, 2 * D_F), np.float32)
    for i in range(BIAS_PIECES):
        for j in range(H_F // 2):
            place[i, j, 2 * j, D_F + i] = 1.0
            place[i, j, 2 * j + 1, i] = 1.0
    return jnp.asarray(place, BF16)


def _forget_bias(lf, batch):
    length = lf.shape[0] // batch
    rows = min(ATTN_TILE, length)
    nt = length // rows
    tri = jnp.asarray(np.tril(np.ones((rows, rows), np.float32)), BF16)
    place = _bias_placement()
    return pl.pallas_call(
        _forget_bias_body, grid=(batch, nt),
        in_specs=[pl.BlockSpec((rows, H_F), lambda b, i: (b * nt + i, 0)), _resident(tri.shape),
                  _resident(place.shape)],
        out_specs=pl.BlockSpec((1, H_F // 2, rows, 2 * D_F), lambda b, i: (b, 0, i, 0)),
        out_shape=jax.ShapeDtypeStruct((batch, H_F // 2, length, 2 * D_F), BF16),
        scratch_shapes=[pltpu.VMEM((1, H_F), F32)],
        name="forget_bias",
    )(lf, tri, place)


def _fox_prefill_body(qi_tab, ki_tab, qt_ref, k_ref, kb_ref, vt_ref, o_ref, qa_scr, m_scr, l_scr, acc_scr, *,
                      heads):
    step_id = pl.program_id(2)
    qi = qi_tab[step_id]
    ki = ki_tab[step_id]
    pair_w = 2 * D_F

    @pl.when(ki == 0)
    def _():
        m_scr[...] = jnp.full(m_scr.shape, NEG_BIG, F32)
        l_scr[...] = jnp.zeros(l_scr.shape, F32)
        acc_scr[...] = jnp.zeros(acc_scr.shape, F32)
        row = lax.broadcasted_iota(jnp.int32, (pair_w, qt_ref.shape[2]), 0)
        for h in range(heads):
            pair = qt_ref[0, (h // 2) * pair_w:(h // 2 + 1) * pair_w, :]
            first = D_F if h % 2 == 0 else 0
            qa_scr[h] = jnp.where((row >= first) & (row < first + BIAS_PIECES), jnp.ones_like(pair), pair)

    def step(masked):
        lane = lax.broadcasted_iota(jnp.int32, (k_ref.shape[0], pair_w), 1)
        for h in range(heads):
            j = h // 2
            own = (lane < D_F) if h % 2 == 0 else (lane >= D_F)
            ka = jnp.where(own, k_ref[:, j * pair_w:(j + 1) * pair_w], kb_ref[0, j])
            st = jnp.dot(ka, qa_scr[h], preferred_element_type=F32)
            if masked:
                key = lax.broadcasted_iota(jnp.int32, st.shape, 0)
                qry = lax.broadcasted_iota(jnp.int32, st.shape, 1)
                st = jnp.where(key <= qry, st, NEG_BIG)
            m_old = m_scr[h]
            m_new = jnp.maximum(m_old, jnp.max(st, axis=0, keepdims=True))
            alpha = jnp.exp2(m_old - m_new)
            pt = jnp.exp2(st - m_new)
            l_scr[h] = alpha * l_scr[h] + jnp.sum(pt, axis=0, keepdims=True)
            rows = slice(h * D_F, (h + 1) * D_F)
            acc_scr[rows, :] = alpha * acc_scr[rows, :] + jnp.dot(vt_ref[0, rows, :], pt.astype(BF16),
                                                                  preferred_element_type=F32)
            m_scr[h] = m_new

    @pl.when(ki < qi)
    def _():
        step(False)

    @pl.when(ki == qi)
    def _():
        step(True)
        out_t = jnp.concatenate([acc_scr[h * D_F:(h + 1) * D_F, :] / l_scr[h] for h in range(heads)], axis=0)
        o_ref[...] = out_t.T.astype(BF16)


def _fox_prefill(fqt, fk, kbias, fvt):
    batch, _, length = fqt.shape
    tile = min(ATTN_TILE, length)
    nt = length // tile
    heads = PREFILL_HEADS
    width = heads * D_F
    pairs = [(q, k) for q in range(nt) for k in range(q + 1)]
    qi_tab = jnp.asarray([q for q, _ in pairs], jnp.int32)
    ki_tab = jnp.asarray([k for _, k in pairs], jnp.int32)
    grid_spec = pltpu.PrefetchScalarGridSpec(
        num_scalar_prefetch=2,
        grid=(batch, H_F // heads, len(pairs)),
        in_specs=[pl.BlockSpec((1, width, tile), lambda b, g, p, qt, kt: (b, g, qt[p])),
                  pl.BlockSpec((tile, width), lambda b, g, p, qt, kt: (b * nt + kt[p], g)),
                  pl.BlockSpec((1, heads // 2, tile, 2 * D_F), lambda b, g, p, qt, kt: (b, g, kt[p], 0)),
                  pl.BlockSpec((1, width, tile), lambda b, g, p, qt, kt: (b, g, kt[p]))],
        out_specs=pl.BlockSpec((tile, width), lambda b, g, p, qt, kt: (b * nt + qt[p], g)),
        scratch_shapes=[pltpu.VMEM((heads, 2 * D_F, tile), BF16), pltpu.VMEM((heads, 1, tile), F32),
                        pltpu.VMEM((heads, 1, tile), F32), pltpu.VMEM((width, tile), F32)],
    )
    return pl.pallas_call(
        functools.partial(_fox_prefill_body, heads=heads),
        grid_spec=grid_spec,
        out_shape=jax.ShapeDtypeStruct(fk.shape, BF16),
        name="fox_prefill",
    )(qi_tab, ki_tab, fqt, fk, kbias, fvt)


def _fox_decode_body(pt_ref, q_ref, kn_ref, vn_ref, lfn_ref, k_hbm, v_hbm, lf_hbm, o_ref,
                     k_buf, v_buf, lf_buf, sems, qbd_scr, m_scr, l_scr, acc_scr, r_scr, *, pages, t_new, steps,
                     total):
    b_idx = pl.program_id(0)
    p_idx = pl.program_id(1)
    n_pages = steps * pages
    rows = H_F * t_new
    half = pages // DECODE_GROUPS
    step_id = b_idx * steps + p_idx
    slot = step_id % DECODE_SLOTS

    def page_copies(step):
        b, p, slot = step // steps, step % steps, step % DECODE_SLOTS
        out = []
        for j in range(pages):
            page = pt_ref[b, n_pages - 1 - (p * pages + j)]
            out.append(pltpu.make_async_copy(k_hbm.at[page], k_buf.at[slot, j], sems.at[0, slot]))
            out.append(pltpu.make_async_copy(v_hbm.at[page], v_buf.at[slot, j], sems.at[1, slot]))
            out.append(pltpu.make_async_copy(lf_hbm.at[page], lf_buf.at[slot, j], sems.at[2, slot]))
        return out

    @pl.when(step_id == 0)
    def _():
        for ahead in range(min(DECODE_SLOTS - 1, total)):
            for c in page_copies(step_id + ahead):
                c.start()

    @pl.when(step_id + DECODE_SLOTS - 1 < total)
    def _():
        for c in page_copies(step_id + DECODE_SLOTS - 1):
            c.start()

    def expand(x):
        return jnp.broadcast_to(x[:, None, :], (H_F, t_new, x.shape[-1])).reshape(rows, x.shape[-1])

    @pl.when(p_idx == 0)
    def _():
        q = jnp.concatenate([q_ref[...].astype(F32)] * H_F, axis=0)
        r_head = lax.broadcasted_iota(jnp.int32, (rows, FQ), 0) // t_new
        l_head = lax.broadcasted_iota(jnp.int32, (rows, FQ), 1) // D_F
        qbd = jnp.where(r_head == l_head, q, 0.0).astype(BF16)
        qbd_scr[...] = qbd
        lfn = lfn_ref[0]
        lane = lax.broadcasted_iota(jnp.int32, lfn.shape, 1)
        cnew = jnp.zeros_like(lfn)
        for i in range(t_new):
            cnew = cnew + jnp.where(lane >= i, lfn[:, i:i + 1], 0.0)
        s = lax.dot_general(qbd, kn_ref[...].astype(BF16), _NT, preferred_element_type=F32) - expand(cnew)
        t_q = lax.broadcasted_iota(jnp.int32, s.shape, 0) % t_new
        t_k = lax.broadcasted_iota(jnp.int32, s.shape, 1)
        s = jnp.where(t_k <= t_q, s, NEG_BIG)
        m = jnp.max(s, axis=-1, keepdims=True)
        p = jnp.exp(s - m)
        m_scr[...] = jnp.full(m_scr.shape, NEG_BIG, F32)
        l_scr[...] = jnp.zeros(l_scr.shape, F32)
        acc_scr[...] = jnp.zeros(acc_scr.shape, F32)
        m_scr[0] = m
        l_scr[0] = jnp.sum(p, axis=-1, keepdims=True)
        acc_scr[0] = jnp.dot(p.astype(BF16), vn_ref[...].astype(BF16), preferred_element_type=F32)
        r_scr[...] = jnp.zeros(r_scr.shape, F32)

    for c in page_copies(step_id):
        c.wait()

    qbd = qbd_scr[...]
    run = r_scr[...]
    biases = []
    for j in range(pages):
        lf = lf_buf[slot, j]
        suffix = _lane_suffix_sum(lf)
        biases.append(expand(suffix - lf + run))
        run = run + suffix[:, 0:1]
    r_scr[...] = run
    for g in range(DECODE_GROUPS):
        js = range(g * half, (g + 1) * half)
        s = jnp.concatenate(
            [jnp.dot(qbd, k_buf[slot, j].reshape(FQ, LANES).astype(BF16), preferred_element_type=F32)
             + biases[j] for j in js], axis=-1)
        m_old = m_scr[g]
        m_new = jnp.maximum(m_old, jnp.max(s, axis=-1, keepdims=True))
        alpha = jnp.exp(m_old - m_new)
        p = jnp.exp(s - m_new)
        l_scr[g] = alpha * l_scr[g] + jnp.sum(p, axis=-1, keepdims=True)
        p = p.astype(BF16)
        acc = alpha * acc_scr[g]
        for i, j in enumerate(js):
            acc = acc + lax.dot_general(p[:, i * LANES:(i + 1) * LANES],
                                        v_buf[slot, j].reshape(FQ, LANES).astype(BF16), _NT,
                                        preferred_element_type=F32)
        acc_scr[g] = acc
        m_scr[g] = m_new

    @pl.when(p_idx == steps - 1)
    def _():
        m = jnp.max(m_scr[...], axis=0)
        scale = jnp.exp(m_scr[...] - m)
        full = jnp.sum(scale * acc_scr[...], axis=0) / jnp.sum(scale * l_scr[...], axis=0)
        l_head = lax.broadcasted_iota(jnp.int32, (t_new, FQ), 1) // D_F
        out = jnp.zeros((t_new, FQ), F32)
        for h in range(H_F):
            out = out + jnp.where(l_head == h, full[h * t_new:(h + 1) * t_new, :], 0.0)
        o_ref[...] = out.astype(o_ref.dtype)


def _fox_decode(fq, fk, fv, lf_new, cache_kt, cache_vt, cache_lft, page_table):
    db, n_pages = page_table.shape
    t_new = fq.shape[0] // db
    pages = math.gcd(PAGES_PER_STEP, n_pages)
    assert pages % DECODE_GROUPS == 0
    steps = n_pages // pages
    rows = H_F * t_new
    page_len = cache_kt.shape[-1]
    assert page_len == LANES
    tok = pl.BlockSpec((t_new, FQ), lambda b, p, pt: (b, 0))
    hbm = pl.BlockSpec(memory_space=pltpu.HBM)
    grid_spec = pltpu.PrefetchScalarGridSpec(
        num_scalar_prefetch=1,
        grid=(db, steps),
        in_specs=[tok, tok, tok, pl.BlockSpec((1, H_F, t_new), lambda b, p, pt: (b, 0, 0)), hbm, hbm, hbm],
        out_specs=tok,
        scratch_shapes=[pltpu.VMEM((DECODE_SLOTS, pages, H_F, D_F, page_len), F32),
                        pltpu.VMEM((DECODE_SLOTS, pages, H_F, D_F, page_len), F32),
                        pltpu.VMEM((DECODE_SLOTS, pages, H_F, page_len), F32),
                        pltpu.SemaphoreType.DMA((3, DECODE_SLOTS)),
                        pltpu.VMEM((rows, FQ), BF16),
                        pltpu.VMEM((DECODE_GROUPS, rows, 1), F32), pltpu.VMEM((DECODE_GROUPS, rows, 1), F32),
                        pltpu.VMEM((DECODE_GROUPS, rows, FQ), F32), pltpu.VMEM((H_F, 1), F32)],
    )
    return pl.pallas_call(
        functools.partial(_fox_decode_body, pages=pages, t_new=t_new, steps=steps, total=db * steps),
        grid_spec=grid_spec,
        out_shape=jax.ShapeDtypeStruct(fq.shape, fq.dtype),
        compiler_params=pltpu.CompilerParams(dimension_semantics=("arbitrary", "arbitrary")),
        name="fox_decode",
    )(page_table, fq, fk, fv, lf_new, cache_kt, cache_vt, cache_lft)


def _mem_attn_body(q_ref, k_ref, v_ref, o_ref):
    n_mem = k_ref.shape[1] // H_M
    outs = []
    for h in range(H_M):
        sl = slice(h * D_M, (h + 1) * D_M)
        head_rows = pl.ds(h, n_mem, stride=H_M)
        s = lax.dot_general(q_ref[:, sl].astype(BF16), k_ref[0, head_rows, :].astype(BF16), _NT,
                            preferred_element_type=F32) * (D_M ** -0.5)
        p = jnp.exp(s - jnp.max(s, axis=-1, keepdims=True))
        l = jnp.sum(p, axis=-1, keepdims=True)
        outs.append(jnp.dot(p.astype(BF16), v_ref[0, head_rows, :].astype(BF16), preferred_element_type=F32) / l)
    o_ref[...] = jnp.concatenate(outs, axis=-1).astype(o_ref.dtype)


def _mem_attn(mq, mem_k, mem_v):
    batch, mem_rows, _ = mem_k.shape
    length = mq.shape[0] // batch
    tq = min(ATTN_TILE, length)
    nt = length // tq
    row = pl.BlockSpec((tq, MQ), lambda b, i: (b * nt + i, 0))
    mem = pl.BlockSpec((1, mem_rows, D_M), lambda b, i: (b, 0, 0))
    return pl.pallas_call(_mem_attn_body, grid=(batch, nt), in_specs=[row, mem, mem], out_specs=row,
                          out_shape=jax.ShapeDtypeStruct(mq.shape, mq.dtype), name="mem_attn")(mq, mem_k, mem_v)


def _mem_kv_body(x_ref, w_ref, k_ref, v_ref):
    tm = x_ref.shape[0]
    kv = jnp.dot(x_ref[...].astype(BF16), w_ref[...], preferred_element_type=F32)
    for h in range(H_M):
        head_rows = pl.ds(h, tm, stride=H_M)
        k_ref[head_rows, :] = kv[:, h * D_M:(h + 1) * D_M]
        v_ref[head_rows, :] = kv[:, MQ + h * D_M:MQ + (h + 1) * D_M]


def _mem_kv(x, w):
    n, d = x.shape
    tm = min(TOKEN_TILE, n)
    out = pl.BlockSpec((tm * H_M, D_M), lambda i: (i, 0))
    shape = jax.ShapeDtypeStruct((n * H_M, D_M), F32)
    return pl.pallas_call(
        _mem_kv_body, grid=(n // tm,),
        in_specs=[pl.BlockSpec((tm, d), lambda i: (i, 0)), _resident(w.shape)],
        out_specs=[out, out], out_shape=[shape, shape], name="mem_kv_proj")(x, w)


def _merge_body(h_ref, ret_ref, fox_ref, mem_ref, gl_ref, w_ret, w_fox, w_mem, w_out, bg_ref, g_ref, b_ref,
                o_ref):
    d = h_ref.shape[1]
    mix = jnp.zeros(h_ref.shape, F32)
    for i, (x_ref, w_ref) in enumerate(((ret_ref, w_ret), (fox_ref, w_fox), (mem_ref, w_mem))):
        gate = jax.nn.sigmoid(gl_ref[:, i * d:(i + 1) * d] + bg_ref[i:i + 1, :])
        mix = mix + gate * jnp.dot(x_ref[...].astype(BF16), w_ref[...], preferred_element_type=F32)
    mixed = jnp.dot(mix.astype(BF16), w_out[...], preferred_element_type=F32)
    o_ref[...] = _layer_norm(ALPHA * h_ref[...] + mixed, g_ref[...], b_ref[...])


def _merge(h, ret, fox, mem, gl, w_ret, w_fox, w_mem, w_out, b_gate, g, b):
    n, d = h.shape
    tm = min(WIDE_TOKEN_TILE, n)
    row = lambda c: pl.BlockSpec((tm, c), lambda i: (i, 0))
    return pl.pallas_call(
        _merge_body, grid=(n // tm,),
        in_specs=[row(d), row(RV), row(FQ), row(MQ), row(N_BRANCH * d),
                  _resident(w_ret.shape), _resident(w_fox.shape), _resident(w_mem.shape), _resident(w_out.shape),
                  _resident(b_gate.shape), _resident((1, d)), _resident((1, d))],
        out_specs=row(d),
        out_shape=jax.ShapeDtypeStruct((n, d), F32), name="merge",
    )(h, ret, fox, mem, gl, w_ret, w_fox, w_mem, w_out, b_gate, g, b)


def _rope_tables(pos):
    half = DK_R // 2
    inv = ROPE_BASE ** (-jnp.arange(half, dtype=F32) / half)
    ang = pos[:, None] * inv[None, :]
    cos, sin = jnp.cos(ang), jnp.sin(ang)
    return (jnp.tile(jnp.concatenate([cos, cos], axis=1), (1, H_R)),
            jnp.tile(jnp.concatenate([-sin, sin], axis=1), (1, H_R)))


def _split_w_in(w_in, d_model, prefill):
    sizes = (RQ, RQ, RV, RV, FQ, FQ, FQ, H_F, MQ, N_BRANCH * d_model)
    names = ("rq", "rk", "rv", "rg", "fq", "fk", "fv", "fl", "mq", "gl")
    offs = np.concatenate([[0], np.cumsum(sizes)])
    w = {nm: w_in[:, int(offs[i]):int(offs[i + 1])].astype(BF16) for i, nm in enumerate(names)}
    if prefill:
        w["fl_nat"] = w["fl"]
        w["fq"], w["fk"], w["fv"] = w["fq"].T, w["fk"].T, w["fv"].T
    w["fl"] = w["fl"].T
    return w


def _layer(x, pos, groups, s0, mem_k, mem_v, fox_attend, prefill, p):
    d = x.shape[1]
    ln_g, ln_b = p["ln_g"], p["ln_b"]
    act_dtype = BF16 if (x.shape[0] // s0.shape[0]) % BF16_ROWS == 0 else F32
    h = _ffn_ln(x, p["wg"][0], p["wu"][0], p["wd"][0], ln_g[0:1], ln_b[0:1])
    cos, sin = _rope_tables(pos)
    rq, rk, rv, rg, mq, gl, *fox_in = _in_proj(
        h, cos, sin, _split_w_in(p["w_in"], d, prefill), p["b_forget"], groups, prefill, act_dtype)
    ret, ret_state = _retention(rq, rk, rv, rg, s0, p["ret_gn_g"])
    fox = fox_attend(*fox_in)
    mem = _mem_attn(mq, mem_k, mem_v)
    h = _merge(h, ret, fox, mem, gl, p["w_ret_o"], p["w_fox_o"], p["w_mem_o"], p["w_out"], p["b_gate"],
               ln_g[1:2], ln_b[1:2])
    y = _ffn_ln(h, p["wg"][1], p["wu"][1], p["wd"][1], ln_g[2:3], ln_b[2:3])
    return y, fox_in, ret_state


def kernel(x_prompt, x_sample, mem_prompt, cache_fox_k, cache_fox_v, cache_fox_logf, state_ret, cache_mem_k, cache_mem_v, page_table, ln_g, ln_b, w_ffn_gate, w_ffn_up, w_ffn_down, w_in, b_forget, b_gate, ret_gn_g, w_ret_o, w_fox_o, w_mem_k, w_mem_v, w_mem_o, w_out):
    bp, lp, d = x_prompt.shape
    db, ls, _ = x_sample.shape
    n_mem = mem_prompt.shape[1]
    n_pages = page_table.shape[1]
    page = cache_fox_k.shape[1]
    past_len = n_pages * page
    params = {
        "ln_g": ln_g, "ln_b": ln_b,
        "wg": [w.astype(BF16) for w in w_ffn_gate], "wu": [w.astype(BF16) for w in w_ffn_up],
        "wd": [w.astype(BF16) for w in w_ffn_down],
        "w_in": w_in, "b_forget": b_forget.reshape(H_F, 1), "b_gate": b_gate,
        "ret_gn_g": ret_gn_g.reshape(1, RV),
        "w_ret_o": w_ret_o.astype(BF16), "w_fox_o": w_fox_o.astype(BF16), "w_mem_o": w_mem_o.astype(BF16),
        "w_out": w_out.astype(BF16),
    }

    mem_k_p, mem_v_p = (t.reshape(bp, n_mem * H_M, D_M) for t in _mem_kv(
        mem_prompt.reshape(bp * n_mem, d), jnp.concatenate([w_mem_k, w_mem_v], axis=1).astype(BF16)))

    def attend_prompt(fqt, fk, fkt, fvt, fvt16, lft, lf):
        return _fox_prefill(fqt, fk, _forget_bias(lf, bp), fvt16)

    y_p, (_, _, fkt_p, fvt_p, _, lft_p, _), ret_p = _layer(
        x_prompt.reshape(bp * lp, d), jnp.arange(lp, dtype=F32), bp, jnp.zeros((bp, H_R, DK_R, DV_R), F32),
        mem_k_p, mem_v_p, attend_prompt, True, params)

    cache_kt = jnp.transpose(cache_fox_k, (0, 2, 3, 1))
    cache_vt = jnp.transpose(cache_fox_v, (0, 2, 3, 1))
    cache_lft = jnp.transpose(cache_fox_logf, (0, 2, 1))

    def attend_sample(fq, fk, fv, lft):
        lf_new = jnp.transpose(lft.reshape(H_F, db, ls), (1, 0, 2))
        return _fox_decode(fq, fk, fv, lf_new, cache_kt, cache_vt, cache_lft, page_table)

    y_s, (_, fk_s, fv_s, lft_s), ret_s = _layer(
        x_sample.reshape(db * ls, d), past_len + jnp.tile(jnp.arange(ls, dtype=F32), db), 1, state_ret,
        cache_mem_k.reshape(db, n_mem * H_M, D_M), cache_mem_v.reshape(db, n_mem * H_M, D_M), attend_sample,
        False, params)

    to_heads = lambda t, b, l: jnp.transpose(t.reshape(b, H_F, D_F, l), (0, 3, 1, 2))
    return (y_p.reshape(bp, lp, d), y_s.reshape(db, ls, d),
            to_heads(fkt_p, bp, lp), to_heads(fvt_p, bp, lp), jnp.transpose(lft_p, (0, 2, 1)),
            ret_p, mem_k_p.reshape(bp, n_mem, H_M, D_M), mem_v_p.reshape(bp, n_mem, H_M, D_M),
            fk_s.reshape(db, ls, H_F, D_F), fv_s.reshape(db, ls, H_F, D_F),
            jnp.transpose(lft_s.reshape(H_F, db, ls), (1, 2, 0)), ret_s)
```

```python
import functools
import math

import jax
import jax.numpy as jnp
import numpy as np
from jax import lax
from jax.experimental import pallas as pl
from jax.experimental.pallas import tpu as pltpu

F32 = jnp.float32
BF16 = jnp.bfloat16

DEPTH = 1
H_R, DK_R, DV_R = 4, 64, 128
RET_CHUNK = 128
RET_CHUNKS_PER_STEP = 4
ROPE_BASE = 10000.0
H_F, D_F = 8, 64
H_M, D_M = 4, 128
N_BRANCH = 3
LN_EPS = 1e-5
GN_EPS = 1e-5
ALPHA = (2.0 * DEPTH) ** 0.25
RQ, RV, FQ, MQ = H_R * DK_R, H_R * DV_R, H_F * D_F, H_M * D_M

LANES = 128
BF16_ROWS = 16
TOKEN_TILE = 256
WIDE_TOKEN_TILE = 512
ATTN_TILE = 512
PREFILL_LOOKAHEAD = 2
PREFILL_HEADS = 8
PAGES_PER_STEP = 16
DECODE_GROUPS = 2
DECODE_SLOTS = 3
NEG_BIG = -1e30
LOG2E = math.log2(math.e)
BIAS_PIECES = 3

_NT = (((1,), (1,)), ((), ()))
_TN = (((0,), (0,)), ((), ()))


def _resident(shape):
    return pl.BlockSpec(shape, lambda *_: (0,) * len(shape), pipeline_mode=pl.Buffered(1))


def _layer_norm(y, g, b):
    mu = jnp.mean(y, axis=-1, keepdims=True)
    yc = y - mu
    var = jnp.mean(yc * yc, axis=-1, keepdims=True)
    return yc * lax.rsqrt(var + LN_EPS) * g + b


def _silu(x):
    return x * jax.nn.sigmoid(x)


def _ffn_ln_body(x_ref, wg_ref, wu_ref, wd_ref, g_ref, b_ref, o_ref):
    x = x_ref[...]
    xb = x.astype(BF16)
    gate = jnp.dot(xb, wg_ref[...], preferred_element_type=F32)
    up = jnp.dot(xb, wu_ref[...], preferred_element_type=F32)
    act = (_silu(gate) * up).astype(BF16)
    f = jnp.dot(act, wd_ref[...], preferred_element_type=F32)
    o_ref[...] = _layer_norm(ALPHA * x + 0.5 * f, g_ref[...], b_ref[...])


def _ffn_ln(x, wg, wu, wd, layer, g, b):
    n, d = x.shape
    d_ff = wg.shape[2]
    tm = min(WIDE_TOKEN_TILE, n)
    stacked = lambda r, c: pl.BlockSpec((None, r, c), lambda i: (layer, 0, 0), pipeline_mode=pl.Buffered(1))
    return pl.pallas_call(
        _ffn_ln_body,
        grid=(n // tm,),
        in_specs=[pl.BlockSpec((tm, d), lambda i: (i, 0)),
                  stacked(d, d_ff), stacked(d, d_ff), stacked(d_ff, d),
                  _resident((1, d)), _resident((1, d))],
        out_specs=pl.BlockSpec((tm, d), lambda i: (i, 0)),
        out_shape=jax.ShapeDtypeStruct((n, d), F32),
        name="ffn_ln",
    )(x, wg, wu, wd, g, b)


def _rotary(x, cos, sin_signed):
    width = x.shape[-1]
    half = DK_R // 2
    lane = lax.broadcasted_iota(jnp.int32, x.shape, 1)
    other = jnp.where((lane % DK_R) < half,
                      pltpu.roll(x, width - half, 1),
                      pltpu.roll(x, half, 1))
    return x * cos + other * sin_signed


def _log_sigmoid(x):
    return jnp.minimum(x, 0.0) - jnp.log1p(jnp.exp(-jnp.abs(x)))


def _in_proj_body(h_ref, cos_ref, sin_ref, w_rq, w_rk, w_rv, w_rg, w_mq, w_gl, w_fq, w_fk, w_fv, w_fl, bf_ref,
                  *rest, prefill):
    if prefill:
        w_fl_nat, bf_row_ref = rest[:2]
        rest = rest[2:]
    rq_o, rk_o, rv_o, rg_o, mq_o, gl_o = rest[:6]
    fox_o = rest[6:]
    hb = h_ref[...].astype(BF16)

    def proj(w):
        return jnp.dot(hb, w, preferred_element_type=F32)

    def proj_t(wt):
        return lax.dot_general(wt, hb, _NT, preferred_element_type=F32)

    cos, sin = cos_ref[...], sin_ref[...]
    rq_o[...] = _rotary(proj(w_rq[...]), cos, sin)
    rk_o[...] = _rotary(proj(w_rk[...]), cos, sin) * (DK_R ** -0.5)
    rv_o[...] = proj(w_rv[...]).astype(rv_o.dtype)
    rg_o[...] = proj(w_rg[...])
    mq_o[...] = proj(w_mq[...]).astype(mq_o.dtype)
    gl_o[...] = proj(w_gl[...])
    if prefill:
        fqt_o, fk_o, fkt_o, fvt_o, fvt16_o, lft_o, lf_o = fox_o
        fqt_o[0] = (proj_t(w_fq[...]) * (D_F ** -0.5 * LOG2E)).astype(BF16)
        fk_o[...] = lax.dot_general(hb, w_fk[...], _NT, preferred_element_type=F32).astype(BF16)
        fkt_o[0] = proj_t(w_fk[...])
        fvt = proj_t(w_fv[...])
        fvt_o[0] = fvt
        fvt16_o[0] = fvt.astype(BF16)
        lft_o[0] = _log_sigmoid(proj_t(w_fl[...]) + bf_ref[...])
        lf_o[...] = _log_sigmoid(proj(w_fl_nat[...]) + bf_row_ref[...])
    else:
        fq_o, fk_o, fv_o, lft_o = fox_o
        fq_o[...] = (proj(w_fq[...]) * (D_F ** -0.5)).astype(fq_o.dtype)
        fk_o[...] = proj(w_fk[...])
        fv_o[...] = proj(w_fv[...])
        lft_o[0] = _log_sigmoid(proj_t(w_fl[...]) + bf_ref[...])


def _in_proj(h, cos, sin, w, b_forget, groups, prefill, act_dtype):
    n, d = h.shape
    lg = n // groups
    tm = min(TOKEN_TILE, lg)
    nt = lg // tm
    row = lambda c: pl.BlockSpec((tm, c), lambda g, i: (g * nt + i, 0))
    tab = pl.BlockSpec((tm, RQ), lambda g, i: (i, 0))
    colt = lambda c: pl.BlockSpec((1, c, tm), lambda g, i: (g, 0, i))
    rows = lambda c, dt: jax.ShapeDtypeStruct((n, c), dt)
    cols = lambda c, dt: jax.ShapeDtypeStruct((groups, c, lg), dt)
    if prefill:
        fox_specs = [colt(FQ), row(FQ), colt(FQ), colt(FQ), colt(FQ), colt(H_F), row(H_F)]
        fox_shapes = [cols(FQ, BF16), rows(FQ, BF16), cols(FQ, F32), cols(FQ, F32), cols(FQ, BF16),
                      cols(H_F, F32), rows(H_F, F32)]
    else:
        fox_specs = [row(FQ), row(FQ), row(FQ), colt(H_F)]
        fox_shapes = [rows(FQ, act_dtype), rows(FQ, F32), rows(FQ, F32), cols(H_F, F32)]
    gates = w["gl"].shape[1]
    weights = [w[k] for k in ("rq", "rk", "rv", "rg", "mq", "gl", "fq", "fk", "fv", "fl")] + [b_forget]
    if prefill:
        weights += [w["fl_nat"], b_forget.reshape(1, H_F)]
    return pl.pallas_call(
        functools.partial(_in_proj_body, prefill=prefill),
        grid=(groups, nt),
        in_specs=[row(d), tab, tab] + [_resident(a.shape) for a in weights],
        out_specs=[row(RQ), row(RQ), row(RV), row(RV), row(MQ), row(gates)] + fox_specs,
        out_shape=[rows(RQ, F32), rows(RQ, F32), rows(RV, act_dtype), rows(RV, F32), rows(MQ, act_dtype),
                   rows(gates, F32)] + fox_shapes,
        name="in_proj",
    )(h, cos, sin, *weights)


def _retention_body(rq_ref, rk_ref, rv_ref, rg_ref, s0_ref, dmask_ref, qdec_ref, kdec_ref, gn_ref,
                    o_ref, sfin_ref, s_scr, *, chunk_decay):
    c = pl.program_id(1)

    @pl.when(c == 0)
    def _():
        s_scr[...] = s0_ref[0]

    chunk = dmask_ref.shape[1]
    for h in range(H_R):
        ks = slice(h * DK_R, (h + 1) * DK_R)
        vs = slice(h * DV_R, (h + 1) * DV_R)
        s_h = s_scr[h]
        for i in range(rq_ref.shape[0] // chunk):
            rs = slice(i * chunk, (i + 1) * chunk)
            q = rq_ref[rs, ks]
            k = rk_ref[rs, ks]
            vh = rv_ref[rs, vs].astype(BF16)
            a = lax.dot_general(q.astype(BF16), k.astype(BF16), _NT,
                                preferred_element_type=F32) * dmask_ref[h]
            o = (jnp.dot(a.astype(BF16), vh, preferred_element_type=F32)
                 + jnp.dot((q * qdec_ref[:, ks]).astype(BF16), s_h.astype(BF16), preferred_element_type=F32))
            s_h = s_h * chunk_decay[h] + lax.dot_general((k * kdec_ref[:, ks]).astype(BF16), vh, _TN,
                                                         preferred_element_type=F32)
            mu = jnp.mean(o, axis=-1, keepdims=True)
            oc = o - mu
            var = jnp.mean(oc * oc, axis=-1, keepdims=True)
            o_ref[rs, vs] = (_silu(rg_ref[rs, vs]) * (oc * lax.rsqrt(var + GN_EPS) * gn_ref[:, vs])
                             ).astype(o_ref.dtype)
        s_scr[h] = s_h

    @pl.when(c == pl.num_programs(1) - 1)
    def _():
        sfin_ref[0] = s_scr[...]


def _retention_tables(chunk):
    log_g = jnp.log1p(-jnp.exp2(-5.0 - jnp.arange(H_R, dtype=F32)))
    i = jnp.arange(chunk, dtype=F32)
    diff = i[:, None] - i[None, :]
    dmask = jnp.where(diff[None] >= 0, jnp.exp(jnp.maximum(diff, 0.0)[None] * log_g[:, None, None]), 0.0)
    q_dec = jnp.exp((i + 1.0)[:, None] * log_g[None, :])
    k_dec = jnp.exp((chunk - 1.0 - i)[:, None] * log_g[None, :])
    widen = lambda t: jnp.repeat(t, DK_R, axis=1)
    return dmask, widen(q_dec), widen(k_dec)


def _chunk_decay(chunk):
    log_g = np.log1p(-np.exp2(-5.0 - np.arange(H_R, dtype=np.float32))).astype(np.float32)
    return tuple(float(v) for v in np.exp(np.float32(chunk) * log_g).astype(np.float32))


def _retention(rq, rk, rv, rg, s0, gn_g):
    batch = s0.shape[0]
    n = rq.shape[0]
    length = n // batch
    chunk = math.gcd(length, RET_CHUNK)
    rows = chunk * math.gcd(length // chunk, RET_CHUNKS_PER_STEP)
    nc = length // rows
    dmask, q_dec, k_dec = _retention_tables(chunk)
    row = lambda c: pl.BlockSpec((rows, c), lambda b, i: (b * nc + i, 0))
    state = pl.BlockSpec((1, H_R, DK_R, DV_R), lambda b, i: (b, 0, 0, 0))
    return pl.pallas_call(
        functools.partial(_retention_body, chunk_decay=_chunk_decay(chunk)),
        grid=(batch, nc),
        in_specs=[row(RQ), row(RQ), row(RV), row(RV), state,
                  _resident((H_R, chunk, chunk)), _resident((chunk, RQ)), _resident((chunk, RQ)),
                  _resident((1, RV))],
        out_specs=[row(RV), state],
        out_shape=[jax.ShapeDtypeStruct((n, RV), rv.dtype),
                   jax.ShapeDtypeStruct((batch, H_R, DK_R, DV_R), F32)],
        scratch_shapes=[pltpu.VMEM((H_R, DK_R, DV_R), F32)],
        name="retention",
    )(rq, rk, rv, rg, s0, dmask, q_dec, k_dec, gn_g)


def _lane_suffix_sum(x):
    lane = lax.broadcasted_iota(jnp.int32, x.shape, 1)
    d = 1
    while d < LANES:
        x = x + jnp.where(lane < LANES - d, pltpu.roll(x, LANES - d, 1), 0.0)
        d *= 2
    return x


def _bf16_pieces(x):
    pieces = []
    for _ in range(BIAS_PIECES):
        piece = x.astype(BF16)
        pieces.append(piece)
        x = x - piece.astype(F32)
    return pieces


def _forget_bias_body(lf_ref, tri_ref, place_ref, o_ref, carry_scr):
    @pl.when(pl.program_id(1) == 0)
    def _():
        carry_scr[...] = jnp.zeros(carry_scr.shape, F32)

    rows = lf_ref.shape[0]
    c = carry_scr[...] + sum(jnp.dot(tri_ref[...], piece, preferred_element_type=F32)
                             for piece in _bf16_pieces(lf_ref[...]))
    carry_scr[...] = c[rows - 1:rows, :]
    for j in range(H_F // 2):
        plane = sum(jnp.dot(piece, place_ref[i, j], preferred_element_type=F32)
                    for i, piece in enumerate(_bf16_pieces(c * (-LOG2E))))
        o_ref[0, j] = plane.astype(BF16)


def _bias_placement():
    place = np.zeros((BIAS_PIECES, H_F // 2, H_F, 2 * D_F), np.float32)
    for i in range(BIAS_PIECES):
        for j in range(H_F // 2):
            place[i, j, 2 * j, D_F + i] = 1.0
            place[i, j, 2 * j + 1, i] = 1.0
    return jnp.asarray(place, BF16)


def _forget_bias(lf, batch):
    length = lf.shape[0] // batch
    rows = min(ATTN_TILE, length)
    nt = length // rows
    tri = jnp.asarray(np.tril(np.ones((rows, rows), np.float32)), BF16)
    place = _bias_placement()
    return pl.pallas_call(
        _forget_bias_body, grid=(batch, nt),
        in_specs=[pl.BlockSpec((rows, H_F), lambda b, i: (b * nt + i, 0)), _resident(tri.shape),
                  _resident(place.shape)],
        out_specs=pl.BlockSpec((1, H_F // 2, rows, 2 * D_F), lambda b, i: (b, 0, i, 0)),
        out_shape=jax.ShapeDtypeStruct((batch, H_F // 2, length, 2 * D_F), BF16),
        scratch_shapes=[pltpu.VMEM((1, H_F), F32)],
        name="forget_bias",
    )(lf, tri, place)


def _fox_prefill_body(qi_tab, ki_tab, qt_ref, k_ref, kb_ref, vt_ref, o_ref, qa_scr, m_scr, l_scr, acc_scr, *,
                      heads):
    step_id = pl.program_id(2)
    qi = qi_tab[step_id]
    ki = ki_tab[step_id]
    pair_w = 2 * D_F

    @pl.when(ki == 0)
    def _():
        m_scr[...] = jnp.full(m_scr.shape, NEG_BIG, F32)
        l_scr[...] = jnp.zeros(l_scr.shape, F32)
        acc_scr[...] = jnp.zeros(acc_scr.shape, F32)
        row = lax.broadcasted_iota(jnp.int32, (pair_w, qt_ref.shape[2]), 0)
        for h in range(heads):
            pair = qt_ref[0, (h // 2) * pair_w:(h // 2 + 1) * pair_w, :]
            first = D_F if h % 2 == 0 else 0
            qa_scr[h] = jnp.where((row >= first) & (row < first + BIAS_PIECES), jnp.ones_like(pair), pair)

    def step(masked):
        lane = lax.broadcasted_iota(jnp.int32, (k_ref.shape[0], pair_w), 1)

        def logits(h):
            j = h // 2
            own = (lane < D_F) if h % 2 == 0 else (lane >= D_F)
            ka = jnp.where(own, k_ref[:, j * pair_w:(j + 1) * pair_w], kb_ref[0, j])
            return jnp.dot(ka, qa_scr[h], preferred_element_type=F32)

        ahead = [logits(h) for h in range(min(PREFILL_LOOKAHEAD, heads))]
        for h in range(heads):
            st = ahead.pop(0)
            if h + PREFILL_LOOKAHEAD < heads:
                ahead.append(logits(h + PREFILL_LOOKAHEAD))
            if masked:
                key = lax.broadcasted_iota(jnp.int32, st.shape, 0)
                qry = lax.broadcasted_iota(jnp.int32, st.shape, 1)
                st = jnp.where(key <= qry, st, NEG_BIG)
            m_old = m_scr[h]
            m_new = jnp.maximum(m_old, jnp.max(st, axis=0, keepdims=True))
            alpha = jnp.exp2(m_old - m_new)
            pt = jnp.exp2(st - m_new)
            l_scr[h] = alpha * l_scr[h] + jnp.sum(pt, axis=0, keepdims=True)
            rows = slice(h * D_F, (h + 1) * D_F)
            acc_scr[rows, :] = alpha * acc_scr[rows, :] + jnp.dot(vt_ref[0, rows, :], pt.astype(BF16),
                                                                  preferred_element_type=F32)
            m_scr[h] = m_new

    @pl.when(ki < qi)
    def _():
        step(False)

    @pl.when(ki == qi)
    def _():
        step(True)
        out_t = jnp.concatenate([acc_scr[h * D_F:(h + 1) * D_F, :] / l_scr[h] for h in range(heads)], axis=0)
        o_ref[...] = out_t.T.astype(BF16)


def _fox_prefill(fqt, fk, kbias, fvt):
    batch, _, length = fqt.shape
    tile = min(ATTN_TILE, length)
    nt = length // tile
    heads = PREFILL_HEADS
    width = heads * D_F
    pairs = [(q, k) for q in range(nt) for k in range(q + 1)]
    qi_tab = jnp.asarray([q for q, _ in pairs], jnp.int32)
    ki_tab = jnp.asarray([k for _, k in pairs], jnp.int32)
    grid_spec = pltpu.PrefetchScalarGridSpec(
        num_scalar_prefetch=2,
        grid=(batch, H_F // heads, len(pairs)),
        in_specs=[pl.BlockSpec((1, width, tile), lambda b, g, p, qt, kt: (b, g, qt[p])),
                  pl.BlockSpec((tile, width), lambda b, g, p, qt, kt: (b * nt + kt[p], g)),
                  pl.BlockSpec((1, heads // 2, tile, 2 * D_F), lambda b, g, p, qt, kt: (b, g, kt[p], 0)),
                  pl.BlockSpec((1, width, tile), lambda b, g, p, qt, kt: (b, g, kt[p]))],
        out_specs=pl.BlockSpec((tile, width), lambda b, g, p, qt, kt: (b * nt + qt[p], g)),
        scratch_shapes=[pltpu.VMEM((heads, 2 * D_F, tile), BF16), pltpu.VMEM((heads, 1, tile), F32),
                        pltpu.VMEM((heads, 1, tile), F32), pltpu.VMEM((width, tile), F32)],
    )
    return pl.pallas_call(
        functools.partial(_fox_prefill_body, heads=heads),
        grid_spec=grid_spec,
        out_shape=jax.ShapeDtypeStruct(fk.shape, BF16),
        name="fox_prefill",
    )(qi_tab, ki_tab, fqt, fk, kbias, fvt)


def _fox_decode_body(pt_ref, q_ref, kn_ref, vn_ref, lfn_ref, k_hbm, v_hbm, lf_hbm, o_ref,
                     k_buf, v_buf, lf_buf, sems, qbd_scr, m_scr, l_scr, acc_scr, r_scr, *, pages, t_new, steps,
                     total):
    b_idx = pl.program_id(0)
    p_idx = pl.program_id(1)
    n_pages = steps * pages
    rows = H_F * t_new
    half = pages // DECODE_GROUPS
    step_id = b_idx * steps + p_idx
    slot = step_id % DECODE_SLOTS

    def page_copies(step):
        b, p, slot = step // steps, step % steps, step % DECODE_SLOTS
        out = []
        for j in range(pages):
            page = pt_ref[b, n_pages - 1 - (p * pages + j)]
            out.append(pltpu.make_async_copy(k_hbm.at[page], k_buf.at[slot, j], sems.at[0, slot]))
            out.append(pltpu.make_async_copy(v_hbm.at[page], v_buf.at[slot, j], sems.at[1, slot]))
            out.append(pltpu.make_async_copy(lf_hbm.at[page], lf_buf.at[slot, j], sems.at[2, slot]))
        return out

    @pl.when(step_id == 0)
    def _():
        for ahead in range(min(DECODE_SLOTS - 1, total)):
            for c in page_copies(step_id + ahead):
                c.start()

    @pl.when(step_id + DECODE_SLOTS - 1 < total)
    def _():
        for c in page_copies(step_id + DECODE_SLOTS - 1):
            c.start()

    def expand(x):
        return jnp.broadcast_to(x[:, None, :], (H_F, t_new, x.shape[-1])).reshape(rows, x.shape[-1])

    @pl.when(p_idx == 0)
    def _():
        q = jnp.concatenate([q_ref[...].astype(F32)] * H_F, axis=0)
        r_head = lax.broadcasted_iota(jnp.int32, (rows, FQ), 0) // t_new
        l_head = lax.broadcasted_iota(jnp.int32, (rows, FQ), 1) // D_F
        qbd = jnp.where(r_head == l_head, q, 0.0).astype(BF16)
        qbd_scr[...] = qbd
        lfn = lfn_ref[0]
        lane = lax.broadcasted_iota(jnp.int32, lfn.shape, 1)
        cnew = jnp.zeros_like(lfn)
        for i in range(t_new):
            cnew = cnew + jnp.where(lane >= i, lfn[:, i:i + 1], 0.0)
        s = lax.dot_general(qbd, kn_ref[...].astype(BF16), _NT, preferred_element_type=F32) - expand(cnew)
        t_q = lax.broadcasted_iota(jnp.int32, s.shape, 0) % t_new
        t_k = lax.broadcasted_iota(jnp.int32, s.shape, 1)
        s = jnp.where(t_k <= t_q, s, NEG_BIG)
        m = jnp.max(s, axis=-1, keepdims=True)
        p = jnp.exp(s - m)
        m_scr[...] = jnp.full(m_scr.shape, NEG_BIG, F32)
        l_scr[...] = jnp.zeros(l_scr.shape, F32)
        acc_scr[...] = jnp.zeros(acc_scr.shape, F32)
        m_scr[0] = m
        l_scr[0] = jnp.sum(p, axis=-1, keepdims=True)
        acc_scr[0] = jnp.dot(p.astype(BF16), vn_ref[...].astype(BF16), preferred_element_type=F32)
        r_scr[...] = jnp.zeros(r_scr.shape, F32)

    for c in page_copies(step_id):
        c.wait()

    qbd = qbd_scr[...]
    run = r_scr[...]
    biases = []
    for j in range(pages):
        lf = lf_buf[slot, j]
        suffix = _lane_suffix_sum(lf)
        biases.append(expand(suffix - lf + run))
        run = run + suffix[:, 0:1]
    r_scr[...] = run
    for g in range(DECODE_GROUPS):
        js = range(g * half, (g + 1) * half)
        s = jnp.concatenate(
            [jnp.dot(qbd, k_buf[slot, j].reshape(FQ, LANES).astype(BF16), preferred_element_type=F32)
             + biases[j] for j in js], axis=-1)
        m_old = m_scr[g]
        m_new = jnp.maximum(m_old, jnp.max(s, axis=-1, keepdims=True))
        alpha = jnp.exp(m_old - m_new)
        p = jnp.exp(s - m_new)
        l_scr[g] = alpha * l_scr[g] + jnp.sum(p, axis=-1, keepdims=True)
        p = p.astype(BF16)
        acc = alpha * acc_scr[g]
        for i, j in enumerate(js):
            acc = acc + lax.dot_general(p[:, i * LANES:(i + 1) * LANES],
                                        v_buf[slot, j].reshape(FQ, LANES).astype(BF16), _NT,
                                        preferred_element_type=F32)
        acc_scr[g] = acc
        m_scr[g] = m_new

    @pl.when(p_idx == steps - 1)
    def _():
        m = jnp.max(m_scr[...], axis=0)
        scale = jnp.exp(m_scr[...] - m)
        full = jnp.sum(scale * acc_scr[...], axis=0) / jnp.sum(scale * l_scr[...], axis=0)
        l_head = lax.broadcasted_iota(jnp.int32, (t_new, FQ), 1) // D_F
        out = jnp.zeros((t_new, FQ), F32)
        for h in range(H_F):
            out = out + jnp.where(l_head == h, full[h * t_new:(h + 1) * t_new, :], 0.0)
        o_ref[...] = out.astype(o_ref.dtype)


def _fox_decode(fq, fk, fv, lf_new, cache_kt, cache_vt, cache_lft, page_table):
    db, n_pages = page_table.shape
    t_new = fq.shape[0] // db
    pages = math.gcd(PAGES_PER_STEP, n_pages)
    assert pages % DECODE_GROUPS == 0
    steps = n_pages // pages
    rows = H_F * t_new
    page_len = cache_kt.shape[-1]
    assert page_len == LANES
    tok = pl.BlockSpec((t_new, FQ), lambda b, p, pt: (b, 0))
    hbm = pl.BlockSpec(memory_space=pltpu.HBM)
    grid_spec = pltpu.PrefetchScalarGridSpec(
        num_scalar_prefetch=1,
        grid=(db, steps),
        in_specs=[tok, tok, tok, pl.BlockSpec((1, H_F, t_new), lambda b, p, pt: (b, 0, 0)), hbm, hbm, hbm],
        out_specs=tok,
        scratch_shapes=[pltpu.VMEM((DECODE_SLOTS, pages, H_F, D_F, page_len), F32),
                        pltpu.VMEM((DECODE_SLOTS, pages, H_F, D_F, page_len), F32),
                        pltpu.VMEM((DECODE_SLOTS, pages, H_F, page_len), F32),
                        pltpu.SemaphoreType.DMA((3, DECODE_SLOTS)),
                        pltpu.VMEM((rows, FQ), BF16),
                        pltpu.VMEM((DECODE_GROUPS, rows, 1), F32), pltpu.VMEM((DECODE_GROUPS, rows, 1), F32),
                        pltpu.VMEM((DECODE_GROUPS, rows, FQ), F32), pltpu.VMEM((H_F, 1), F32)],
    )
    return pl.pallas_call(
        functools.partial(_fox_decode_body, pages=pages, t_new=t_new, steps=steps, total=db * steps),
        grid_spec=grid_spec,
        out_shape=jax.ShapeDtypeStruct(fq.shape, fq.dtype),
        compiler_params=pltpu.CompilerParams(dimension_semantics=("arbitrary", "arbitrary")),
        name="fox_decode",
    )(page_table, fq, fk, fv, lf_new, cache_kt, cache_vt, cache_lft)


def _mem_attn_body(q_ref, k_ref, v_ref, o_ref):
    n_mem = k_ref.shape[1] // H_M
    outs = []
    for h in range(H_M):
        sl = slice(h * D_M, (h + 1) * D_M)
        head_rows = pl.ds(h, n_mem, stride=H_M)
        s = lax.dot_general(q_ref[:, sl].astype(BF16), k_ref[0, head_rows, :].astype(BF16), _NT,
                            preferred_element_type=F32) * (D_M ** -0.5)
        p = jnp.exp(s - jnp.max(s, axis=-1, keepdims=True))
        l = jnp.sum(p, axis=-1, keepdims=True)
        outs.append(jnp.dot(p.astype(BF16), v_ref[0, head_rows, :].astype(BF16), preferred_element_type=F32) / l)
    o_ref[...] = jnp.concatenate(outs, axis=-1).astype(o_ref.dtype)


def _mem_attn(mq, mem_k, mem_v):
    batch, mem_rows, _ = mem_k.shape
    length = mq.shape[0] // batch
    tq = min(ATTN_TILE, length)
    nt = length // tq
    row = pl.BlockSpec((tq, MQ), lambda b, i: (b * nt + i, 0))
    mem = pl.BlockSpec((1, mem_rows, D_M), lambda b, i: (b, 0, 0))
    return pl.pallas_call(_mem_attn_body, grid=(batch, nt), in_specs=[row, mem, mem], out_specs=row,
                          out_shape=jax.ShapeDtypeStruct(mq.shape, mq.dtype), name="mem_attn")(mq, mem_k, mem_v)


def _mem_kv_body(x_ref, w_ref, k_ref, v_ref):
    tm = x_ref.shape[0]
    kv = jnp.dot(x_ref[...].astype(BF16), w_ref[...], preferred_element_type=F32)
    for h in range(H_M):
        head_rows = pl.ds(h, tm, stride=H_M)
        k_ref[head_rows, :] = kv[:, h * D_M:(h + 1) * D_M]
        v_ref[head_rows, :] = kv[:, MQ + h * D_M:MQ + (h + 1) * D_M]


def _mem_kv(x, w):
    n, d = x.shape
    tm = min(TOKEN_TILE, n)
    out = pl.BlockSpec((tm * H_M, D_M), lambda i: (i, 0))
    shape = jax.ShapeDtypeStruct((n * H_M, D_M), F32)
    return pl.pallas_call(
        _mem_kv_body, grid=(n // tm,),
        in_specs=[pl.BlockSpec((tm, d), lambda i: (i, 0)), _resident(w.shape)],
        out_specs=[out, out], out_shape=[shape, shape], name="mem_kv_proj")(x, w)


def _merge_body(h_ref, ret_ref, fox_ref, mem_ref, gl_ref, w_ret, w_fox, w_mem, w_out, bg_ref, g_ref, b_ref,
                o_ref):
    d = h_ref.shape[1]
    mix = jnp.zeros(h_ref.shape, F32)
    for i, (x_ref, w_ref) in enumerate(((ret_ref, w_ret), (fox_ref, w_fox), (mem_ref, w_mem))):
        gate = jax.nn.sigmoid(gl_ref[:, i * d:(i + 1) * d] + bg_ref[i:i + 1, :])
        mix = mix + gate * jnp.dot(x_ref[...].astype(BF16), w_ref[...], preferred_element_type=F32)
    mixed = jnp.dot(mix.astype(BF16), w_out[...], preferred_element_type=F32)
    o_ref[...] = _layer_norm(ALPHA * h_ref[...] + mixed, g_ref[...], b_ref[...])


def _merge(h, ret, fox, mem, gl, w_ret, w_fox, w_mem, w_out, b_gate, g, b):
    n, d = h.shape
    tm = min(WIDE_TOKEN_TILE, n)
    row = lambda c: pl.BlockSpec((tm, c), lambda i: (i, 0))
    return pl.pallas_call(
        _merge_body, grid=(n // tm,),
        in_specs=[row(d), row(RV), row(FQ), row(MQ), row(N_BRANCH * d),
                  _resident(w_ret.shape), _resident(w_fox.shape), _resident(w_mem.shape), _resident(w_out.shape),
                  _resident(b_gate.shape), _resident((1, d)), _resident((1, d))],
        out_specs=row(d),
        out_shape=jax.ShapeDtypeStruct((n, d), F32), name="merge",
    )(h, ret, fox, mem, gl, w_ret, w_fox, w_mem, w_out, b_gate, g, b)


def _rope_tables(pos):
    half = DK_R // 2
    inv = ROPE_BASE ** (-jnp.arange(half, dtype=F32) / half)
    ang = pos[:, None] * inv[None, :]
    cos, sin = jnp.cos(ang), jnp.sin(ang)
    return (jnp.tile(jnp.concatenate([cos, cos], axis=1), (1, H_R)),
            jnp.tile(jnp.concatenate([-sin, sin], axis=1), (1, H_R)))


def _split_w_in(w_in, d_model, prefill):
    sizes = (RQ, RQ, RV, RV, FQ, FQ, FQ, H_F, MQ, N_BRANCH * d_model)
    names = ("rq", "rk", "rv", "rg", "fq", "fk", "fv", "fl", "mq", "gl")
    offs = np.concatenate([[0], np.cumsum(sizes)])
    w = {nm: w_in[:, int(offs[i]):int(offs[i + 1])].astype(BF16) for i, nm in enumerate(names)}
    if prefill:
        w["fl_nat"] = w["fl"]
        w["fq"], w["fk"], w["fv"] = w["fq"].T, w["fk"].T, w["fv"].T
    w["fl"] = w["fl"].T
    return w


def _layer(x, pos, groups, s0, mem_k, mem_v, fox_attend, prefill, p):
    d = x.shape[1]
    ln_g, ln_b = p["ln_g"], p["ln_b"]
    act_dtype = BF16 if (x.shape[0] // s0.shape[0]) % BF16_ROWS == 0 else F32
    h = _ffn_ln(x, p["wg"], p["wu"], p["wd"], 0, ln_g[0:1], ln_b[0:1])
    cos, sin = _rope_tables(pos)
    rq, rk, rv, rg, mq, gl, *fox_in = _in_proj(
        h, cos, sin, _split_w_in(p["w_in"], d, prefill), p["b_forget"], groups, prefill, act_dtype)
    ret, ret_state = _retention(rq, rk, rv, rg, s0, p["ret_gn_g"])
    fox = fox_attend(*fox_in)
    mem = _mem_attn(mq, mem_k, mem_v)
    h = _merge(h, ret, fox, mem, gl, p["w_ret_o"], p["w_fox_o"], p["w_mem_o"], p["w_out"], p["b_gate"],
               ln_g[1:2], ln_b[1:2])
    y = _ffn_ln(h, p["wg"], p["wu"], p["wd"], 1, ln_g[2:3], ln_b[2:3])
    return y, fox_in, ret_state


def kernel(x_prompt, x_sample, mem_prompt, cache_fox_k, cache_fox_v, cache_fox_logf, state_ret, cache_mem_k, cache_mem_v, page_table, ln_g, ln_b, w_ffn_gate, w_ffn_up, w_ffn_down, w_in, b_forget, b_gate, ret_gn_g, w_ret_o, w_fox_o, w_mem_k, w_mem_v, w_mem_o, w_out):
    bp, lp, d = x_prompt.shape
    db, ls, _ = x_sample.shape
    n_mem = mem_prompt.shape[1]
    n_pages = page_table.shape[1]
    page = cache_fox_k.shape[1]
    past_len = n_pages * page
    params = {
        "ln_g": ln_g, "ln_b": ln_b,
        "wg": w_ffn_gate.astype(BF16), "wu": w_ffn_up.astype(BF16), "wd": w_ffn_down.astype(BF16),
        "w_in": w_in, "b_forget": b_forget.reshape(H_F, 1), "b_gate": b_gate,
        "ret_gn_g": ret_gn_g.reshape(1, RV),
        "w_ret_o": w_ret_o.astype(BF16), "w_fox_o": w_fox_o.astype(BF16), "w_mem_o": w_mem_o.astype(BF16),
        "w_out": w_out.astype(BF16),
    }

    mem_k_p, mem_v_p = (t.reshape(bp, n_mem * H_M, D_M) for t in _mem_kv(
        mem_prompt.reshape(bp * n_mem, d), jnp.concatenate([w_mem_k, w_mem_v], axis=1).astype(BF16)))

    def attend_prompt(fqt, fk, fkt, fvt, fvt16, lft, lf):
        return _fox_prefill(fqt, fk, _forget_bias(lf, bp), fvt16)

    y_p, (_, _, fkt_p, fvt_p, _, lft_p, _), ret_p = _layer(
        x_prompt.reshape(bp * lp, d), jnp.arange(lp, dtype=F32), bp, jnp.zeros((bp, H_R, DK_R, DV_R), F32),
        mem_k_p, mem_v_p, attend_prompt, True, params)

    cache_kt = jnp.transpose(cache_fox_k, (0, 2, 3, 1))
    cache_vt = jnp.transpose(cache_fox_v, (0, 2, 3, 1))
    cache_lft = jnp.transpose(cache_fox_logf, (0, 2, 1))

    def attend_sample(fq, fk, fv, lft):
        lf_new = jnp.transpose(lft.reshape(H_F, db, ls), (1, 0, 2))
        return _fox_decode(fq, fk, fv, lf_new, cache_kt, cache_vt, cache_lft, page_table)

    y_s, (_, fk_s, fv_s, lft_s), ret_s = _layer(
        x_sample.reshape(db * ls, d), past_len + jnp.tile(jnp.arange(ls, dtype=F32), db), 1, state_ret,
        cache_mem_k.reshape(db, n_mem * H_M, D_M), cache_mem_v.reshape(db, n_mem * H_M, D_M), attend_sample,
        False, params)

    to_heads = lambda t, b, l: jnp.transpose(t.reshape(b, H_F, D_F, l), (0, 3, 1, 2))
    return (y_p.reshape(bp, lp, d), y_s.reshape(db, ls, d),
            to_heads(fkt_p, bp, lp), to_heads(fvt_p, bp, lp), jnp.transpose(lft_p, (0, 2, 1)),
            ret_p, mem_k_p.reshape(bp, n_mem, H_M, D_M), mem_v_p.reshape(bp, n_mem, H_M, D_M),
            fk_s.reshape(db, ls, H_F, D_F), fv_s.reshape(db, ls, H_F, D_F),
            jnp.transpose(lft_s.reshape(H_F, db, ls), (1, 2, 0)), ret_s)
```

```python
import functools
import math

import jax
import jax.numpy as jnp
import numpy as np
from jax import lax
from jax.experimental import pallas as pl
from jax.experimental.pallas import tpu as pltpu

F32 = jnp.float32
BF16 = jnp.bfloat16

DEPTH = 1
H_R, DK_R, DV_R = 4, 64, 128
RET_CHUNK = 128
RET_CHUNKS_PER_STEP = 4
ROPE_BASE = 10000.0
H_F, D_F = 8, 64
H_M, D_M = 4, 128
N_BRANCH = 3
LN_EPS = 1e-5
GN_EPS = 1e-5
ALPHA = (2.0 * DEPTH) ** 0.25
RQ, RV, FQ, MQ = H_R * DK_R, H_R * DV_R, H_F * D_F, H_M * D_M

LANES = 128
BF16_ROWS = 16
TOKEN_TILE = 256
WIDE_TOKEN_TILE = 512
ATTN_TILE = 512
PREFILL_LOOKAHEAD = 2
PREFILL_HEADS = 8
PAGES_PER_STEP = 16
DECODE_GROUPS = 2
DECODE_SLOTS = 3
NEG_BIG = -1e30
LOG2E = math.log2(math.e)
BIAS_PIECES = 3

_NT = (((1,), (1,)), ((), ()))
_TN = (((0,), (0,)), ((), ()))


def _resident(shape):
    return pl.BlockSpec(shape, lambda *_: (0,) * len(shape), pipeline_mode=pl.Buffered(1))


def _layer_norm(y, g, b):
    mu = jnp.mean(y, axis=-1, keepdims=True)
    yc = y - mu
    var = jnp.mean(yc * yc, axis=-1, keepdims=True)
    return yc * lax.rsqrt(var + LN_EPS) * g + b


def _silu(x):
    return x * jax.nn.sigmoid(x)


def _ffn_ln_body(x_ref, wg_ref, wu_ref, wd_ref, g_ref, b_ref, o_ref):
    x = x_ref[...]
    xb = x.astype(BF16)
    gate = jnp.dot(xb, wg_ref[...], preferred_element_type=F32)
    up = jnp.dot(xb, wu_ref[...], preferred_element_type=F32)
    act = (_silu(gate) * up).astype(BF16)
    f = jnp.dot(act, wd_ref[...], preferred_element_type=F32)
    o_ref[...] = _layer_norm(ALPHA * x + 0.5 * f, g_ref[...], b_ref[...])


def _ffn_ln(x, wg, wu, wd, layer, g, b):
    n, d = x.shape
    d_ff = wg.shape[2]
    tm = min(WIDE_TOKEN_TILE, n)
    stacked = lambda r, c: pl.BlockSpec((None, r, c), lambda i: (layer, 0, 0), pipeline_mode=pl.Buffered(1))
    return pl.pallas_call(
        _ffn_ln_body,
        grid=(n // tm,),
        in_specs=[pl.BlockSpec((tm, d), lambda i: (i, 0)),
                  stacked(d, d_ff), stacked(d, d_ff), stacked(d_ff, d),
                  _resident((1, d)), _resident((1, d))],
        out_specs=pl.BlockSpec((tm, d), lambda i: (i, 0)),
        out_shape=jax.ShapeDtypeStruct((n, d), F32),
        name="ffn_ln",
    )(x, wg, wu, wd, g, b)


def _rotary(x, cos, sin_signed):
    width = x.shape[-1]
    half = DK_R // 2
    lane = lax.broadcasted_iota(jnp.int32, x.shape, 1)
    other = jnp.where((lane % DK_R) < half,
                      pltpu.roll(x, width - half, 1),
                      pltpu.roll(x, half, 1))
    return x * cos + other * sin_signed


def _log_sigmoid(x):
    return jnp.minimum(x, 0.0) - jnp.log1p(jnp.exp(-jnp.abs(x)))


def _in_proj_body(h_ref, cos_ref, sin_ref, w_rq, w_rk, w_rv, w_rg, w_mq, w_gl, w_fq, w_fk, w_fv, w_fl, bf_ref,
                  *rest, prefill):
    if prefill:
        w_fl_nat, bf_row_ref = rest[:2]
        rest = rest[2:]
    rq_o, rk_o, rv_o, rg_o, mq_o, gl_o = rest[:6]
    fox_o = rest[6:]
    hb = h_ref[...].astype(BF16)

    def proj(w):
        return jnp.dot(hb, w, preferred_element_type=F32)

    def proj_t(wt):
        return lax.dot_general(wt, hb, _NT, preferred_element_type=F32)

    cos, sin = cos_ref[...], sin_ref[...]
    rq_o[...] = _rotary(proj(w_rq[...]), cos, sin)
    rk_o[...] = _rotary(proj(w_rk[...]), cos, sin) * (DK_R ** -0.5)
    rv_o[...] = proj(w_rv[...]).astype(rv_o.dtype)
    rg_o[...] = proj(w_rg[...])
    mq_o[...] = proj(w_mq[...]).astype(mq_o.dtype)
    gl_o[...] = proj(w_gl[...])
    if prefill:
        fqt_o, fk_o, fkt_o, fvt_o, fvt16_o, lft_o, lf_o = fox_o
        fqt_o[0] = (proj_t(w_fq[...]) * (D_F ** -0.5 * LOG2E)).astype(BF16)
        fk_o[...] = lax.dot_general(hb, w_fk[...], _NT, preferred_element_type=F32).astype(BF16)
        fkt_o[0] = proj_t(w_fk[...])
        fvt = proj_t(w_fv[...])
        fvt_o[0] = fvt
        fvt16_o[0] = fvt.astype(BF16)
        lft_o[0] = _log_sigmoid(proj_t(w_fl[...]) + bf_ref[...])
        lf_o[...] = _log_sigmoid(proj(w_fl_nat[...]) + bf_row_ref[...])
    else:
        fq_o, fk_o, fv_o, lft_o = fox_o
        fq_o[...] = (proj(w_fq[...]) * (D_F ** -0.5)).astype(fq_o.dtype)
        fk_o[...] = proj(w_fk[...])
        fv_o[...] = proj(w_fv[...])
        lft_o[0] = _log_sigmoid(proj_t(w_fl[...]) + bf_ref[...])


def _in_proj(h, cos, sin, w, b_forget, groups, prefill, act_dtype):
    n, d = h.shape
    lg = n // groups
    tm = min(TOKEN_TILE, lg)
    nt = lg // tm
    row = lambda c: pl.BlockSpec((tm, c), lambda g, i: (g * nt + i, 0))
    tab = pl.BlockSpec((tm, RQ), lambda g, i: (i, 0))
    colt = lambda c: pl.BlockSpec((1, c, tm), lambda g, i: (g, 0, i))
    rows = lambda c, dt: jax.ShapeDtypeStruct((n, c), dt)
    cols = lambda c, dt: jax.ShapeDtypeStruct((groups, c, lg), dt)
    if prefill:
        fox_specs = [colt(FQ), row(FQ), colt(FQ), colt(FQ), colt(FQ), colt(H_F), row(H_F)]
        fox_shapes = [cols(FQ, BF16), rows(FQ, BF16), cols(FQ, F32), cols(FQ, F32), cols(FQ, BF16),
                      cols(H_F, F32), rows(H_F, F32)]
    else:
        fox_specs = [row(FQ), row(FQ), row(FQ), colt(H_F)]
        fox_shapes = [rows(FQ, act_dtype), rows(FQ, F32), rows(FQ, F32), cols(H_F, F32)]
    gates = w["gl"].shape[1]
    weights = [w[k] for k in ("rq", "rk", "rv", "rg", "mq", "gl", "fq", "fk", "fv", "fl")] + [b_forget]
    if prefill:
        weights += [w["fl_nat"], b_forget.reshape(1, H_F)]
    return pl.pallas_call(
        functools.partial(_in_proj_body, prefill=prefill),
        grid=(groups, nt),
        in_specs=[row(d), tab, tab] + [_resident(a.shape) for a in weights],
        out_specs=[row(RQ), row(RQ), row(RV), row(RV), row(MQ), row(gates)] + fox_specs,
        out_shape=[rows(RQ, F32), rows(RQ, F32), rows(RV, act_dtype), rows(RV, F32), rows(MQ, act_dtype),
                   rows(gates, F32)] + fox_shapes,
        name="in_proj",
    )(h, cos, sin, *weights)


def _retention_body(rq_ref, rk_ref, rv_ref, rg_ref, s0_ref, dmask_ref, qdec_ref, kdec_ref, gn_ref,
                    o_ref, sfin_ref, s_scr, *, chunk_decay):
    c = pl.program_id(1)

    @pl.when(c == 0)
    def _():
        s_scr[...] = s0_ref[0]

    chunk = dmask_ref.shape[1]
    for h in range(H_R):
        ks = slice(h * DK_R, (h + 1) * DK_R)
        vs = slice(h * DV_R, (h + 1) * DV_R)
        s_h = s_scr[h]
        n_sub = rq_ref.shape[0] // chunk
        intra, kv = [], []
        for i in range(n_sub):
            rs = slice(i * chunk, (i + 1) * chunk)
            q = rq_ref[rs, ks]
            k = rk_ref[rs, ks]
            vh = rv_ref[rs, vs].astype(BF16)
            a = lax.dot_general(q.astype(BF16), k.astype(BF16), _NT,
                                preferred_element_type=F32) * dmask_ref[h]
            intra.append(jnp.dot(a.astype(BF16), vh, preferred_element_type=F32))
            kv.append(lax.dot_general((k * kdec_ref[:, ks]).astype(BF16), vh, _TN, preferred_element_type=F32))
        for i in range(n_sub):
            rs = slice(i * chunk, (i + 1) * chunk)
            o = intra[i] + jnp.dot((rq_ref[rs, ks] * qdec_ref[:, ks]).astype(BF16), s_h.astype(BF16),
                                   preferred_element_type=F32)
            s_h = s_h * chunk_decay[h] + kv[i]
            mu = jnp.mean(o, axis=-1, keepdims=True)
            oc = o - mu
            var = jnp.mean(oc * oc, axis=-1, keepdims=True)
            o_ref[rs, vs] = (_silu(rg_ref[rs, vs]) * (oc * lax.rsqrt(var + GN_EPS) * gn_ref[:, vs])
                             ).astype(o_ref.dtype)
        s_scr[h] = s_h

    @pl.when(c == pl.num_programs(1) - 1)
    def _():
        sfin_ref[0] = s_scr[...]


def _retention_tables(chunk):
    log_g = jnp.log1p(-jnp.exp2(-5.0 - jnp.arange(H_R, dtype=F32)))
    i = jnp.arange(chunk, dtype=F32)
    diff = i[:, None] - i[None, :]
    dmask = jnp.where(diff[None] >= 0, jnp.exp(jnp.maximum(diff, 0.0)[None] * log_g[:, None, None]), 0.0)
    q_dec = jnp.exp((i + 1.0)[:, None] * log_g[None, :])
    k_dec = jnp.exp((chunk - 1.0 - i)[:, None] * log_g[None, :])
    widen = lambda t: jnp.repeat(t, DK_R, axis=1)
    return dmask, widen(q_dec), widen(k_dec)


def _chunk_decay(chunk):
    log_g = np.log1p(-np.exp2(-5.0 - np.arange(H_R, dtype=np.float32))).astype(np.float32)
    return tuple(float(v) for v in np.exp(np.float32(chunk) * log_g).astype(np.float32))


def _retention(rq, rk, rv, rg, s0, gn_g):
    batch = s0.shape[0]
    n = rq.shape[0]
    length = n // batch
    chunk = math.gcd(length, RET_CHUNK)
    rows = chunk * math.gcd(length // chunk, RET_CHUNKS_PER_STEP)
    nc = length // rows
    dmask, q_dec, k_dec = _retention_tables(chunk)
    row = lambda c: pl.BlockSpec((rows, c), lambda b, i: (b * nc + i, 0))
    state = pl.BlockSpec((1, H_R, DK_R, DV_R), lambda b, i: (b, 0, 0, 0))
    return pl.pallas_call(
        functools.partial(_retention_body, chunk_decay=_chunk_decay(chunk)),
        grid=(batch, nc),
        in_specs=[row(RQ), row(RQ), row(RV), row(RV), state,
                  _resident((H_R, chunk, chunk)), _resident((chunk, RQ)), _resident((chunk, RQ)),
                  _resident((1, RV))],
        out_specs=[row(RV), state],
        out_shape=[jax.ShapeDtypeStruct((n, RV), rv.dtype),
                   jax.ShapeDtypeStruct((batch, H_R, DK_R, DV_R), F32)],
        scratch_shapes=[pltpu.VMEM((H_R, DK_R, DV_R), F32)],
        name="retention",
    )(rq, rk, rv, rg, s0, dmask, q_dec, k_dec, gn_g)


def _lane_suffix_sum(x):
    lane = lax.broadcasted_iota(jnp.int32, x.shape, 1)
    d = 1
    while d < LANES:
        x = x + jnp.where(lane < LANES - d, pltpu.roll(x, LANES - d, 1), 0.0)
        d *= 2
    return x


def _bf16_pieces(x):
    pieces = []
    for _ in range(BIAS_PIECES):
        piece = x.astype(BF16)
        pieces.append(piece)
        x = x - piece.astype(F32)
    return pieces


def _forget_bias_body(lf_ref, tri_ref, place_ref, o_ref, carry_scr):
    @pl.when(pl.program_id(1) == 0)
    def _():
        carry_scr[...] = jnp.zeros(carry_scr.shape, F32)

    rows = lf_ref.shape[0]
    c = carry_scr[...] + sum(jnp.dot(tri_ref[...], piece, preferred_element_type=F32)
                             for piece in _bf16_pieces(lf_ref[...]))
    carry_scr[...] = c[rows - 1:rows, :]
    planes = sum(jnp.dot(piece, place_ref[i], preferred_element_type=F32)
                 for i, piece in enumerate(_bf16_pieces(c * (-LOG2E))))
    for j in range(H_F // 2):
        o_ref[0, j] = planes[:, j * 2 * D_F:(j + 1) * 2 * D_F].astype(BF16)


def _bias_placement():
    plane_w = 2 * D_F
    place = np.zeros((BIAS_PIECES, H_F, (H_F // 2) * plane_w), np.float32)
    for i in range(BIAS_PIECES):
        for j in range(H_F // 2):
            place[i, 2 * j, j * plane_w + D_F + i] = 1.0
            place[i, 2 * j + 1, j * plane_w + i] = 1.0
    return jnp.asarray(place, BF16)


def _forget_bias(lf, batch):
    length = lf.shape[0] // batch
    rows = min(ATTN_TILE, length)
    nt = length // rows
    tri = jnp.asarray(np.tril(np.ones((rows, rows), np.float32)), BF16)
    place = _bias_placement()
    return pl.pallas_call(
        _forget_bias_body, grid=(batch, nt),
        in_specs=[pl.BlockSpec((rows, H_F), lambda b, i: (b * nt + i, 0)), _resident(tri.shape),
                  _resident(place.shape)],
        out_specs=pl.BlockSpec((1, H_F // 2, rows, 2 * D_F), lambda b, i: (b, 0, i, 0)),
        out_shape=jax.ShapeDtypeStruct((batch, H_F // 2, length, 2 * D_F), BF16),
        scratch_shapes=[pltpu.VMEM((1, H_F), F32)],
        name="forget_bias",
    )(lf, tri, place)


def _fox_prefill_body(qi_tab, ki_tab, qt_ref, k_ref, kb_ref, vt_ref, o_ref, qa_scr, m_scr, acc_scr, *, heads):
    step_id = pl.program_id(2)
    qi = qi_tab[step_id]
    ki = ki_tab[step_id]
    pair_w = 2 * D_F
    acc_rows = D_F + BF16_ROWS

    @pl.when(ki == 0)
    def _():
        m_scr[...] = jnp.full(m_scr.shape, NEG_BIG, F32)
        acc_scr[...] = jnp.zeros(acc_scr.shape, F32)
        row = lax.broadcasted_iota(jnp.int32, (pair_w, qt_ref.shape[2]), 0)
        for h in range(heads):
            pair = qt_ref[0, (h // 2) * pair_w:(h // 2 + 1) * pair_w, :]
            first = D_F if h % 2 == 0 else 0
            qa_scr[h] = jnp.where((row >= first) & (row < first + BIAS_PIECES), jnp.ones_like(pair), pair)

    def step(masked):
        lane = lax.broadcasted_iota(jnp.int32, (k_ref.shape[0], pair_w), 1)
        ones = jnp.ones((BF16_ROWS, k_ref.shape[0]), BF16)

        def logits(h):
            j = h // 2
            own = (lane < D_F) if h % 2 == 0 else (lane >= D_F)
            ka = jnp.where(own, k_ref[:, j * pair_w:(j + 1) * pair_w], kb_ref[0, j])
            return jnp.dot(ka, qa_scr[h], preferred_element_type=F32)

        ahead = [logits(h) for h in range(min(PREFILL_LOOKAHEAD, heads))]
        for h in range(heads):
            st = ahead.pop(0)
            if h + PREFILL_LOOKAHEAD < heads:
                ahead.append(logits(h + PREFILL_LOOKAHEAD))
            if masked:
                key = lax.broadcasted_iota(jnp.int32, st.shape, 0)
                qry = lax.broadcasted_iota(jnp.int32, st.shape, 1)
                st = jnp.where(key <= qry, st, NEG_BIG)
            m_old = m_scr[h]
            m_new = jnp.maximum(m_old, jnp.max(st, axis=0, keepdims=True))
            alpha = jnp.exp2(m_old - m_new)
            pt = jnp.exp2(st - m_new)
            va = jnp.concatenate([vt_ref[0, h * D_F:(h + 1) * D_F, :], ones], axis=0)
            rows = slice(h * acc_rows, (h + 1) * acc_rows)
            acc_scr[rows, :] = alpha * acc_scr[rows, :] + jnp.dot(va, pt.astype(BF16),
                                                                  preferred_element_type=F32)
            m_scr[h] = m_new

    @pl.when(ki < qi)
    def _():
        step(False)

    @pl.when(ki == qi)
    def _():
        step(True)
        out_t = jnp.concatenate([acc_scr[h * acc_rows:h * acc_rows + D_F, :]
                                 / acc_scr[h * acc_rows + D_F:h * acc_rows + D_F + 1, :] for h in range(heads)],
                                axis=0)
        o_ref[...] = out_t.T.astype(BF16)


def _fox_prefill(fqt, fk, kbias, fvt):
    batch, _, length = fqt.shape
    tile = min(ATTN_TILE, length)
    nt = length // tile
    heads = PREFILL_HEADS
    width = heads * D_F
    pairs = [(q, k) for q in range(nt) for k in range(q + 1)]
    qi_tab = jnp.asarray([q for q, _ in pairs], jnp.int32)
    ki_tab = jnp.asarray([k for _, k in pairs], jnp.int32)
    grid_spec = pltpu.PrefetchScalarGridSpec(
        num_scalar_prefetch=2,
        grid=(batch, H_F // heads, len(pairs)),
        in_specs=[pl.BlockSpec((1, width, tile), lambda b, g, p, qt, kt: (b, g, qt[p])),
                  pl.BlockSpec((tile, width), lambda b, g, p, qt, kt: (b * nt + kt[p], g)),
                  pl.BlockSpec((1, heads // 2, tile, 2 * D_F), lambda b, g, p, qt, kt: (b, g, kt[p], 0)),
                  pl.BlockSpec((1, width, tile), lambda b, g, p, qt, kt: (b, g, kt[p]))],
        out_specs=pl.BlockSpec((tile, width), lambda b, g, p, qt, kt: (b * nt + qt[p], g)),
        scratch_shapes=[pltpu.VMEM((heads, 2 * D_F, tile), BF16), pltpu.VMEM((heads, 1, tile), F32),
                        pltpu.VMEM((heads * (D_F + BF16_ROWS), tile), F32)],
    )
    return pl.pallas_call(
        functools.partial(_fox_prefill_body, heads=heads),
        grid_spec=grid_spec,
        out_shape=jax.ShapeDtypeStruct(fk.shape, BF16),
        name="fox_prefill",
    )(qi_tab, ki_tab, fqt, fk, kbias, fvt)


def _fox_decode_body(pt_ref, q_ref, kn_ref, vn_ref, lfn_ref, k_hbm, v_hbm, lf_hbm, o_ref,
                     k_buf, v_buf, lf_buf, sems, qbd_scr, m_scr, l_scr, acc_scr, r_scr, *, pages, t_new, steps,
                     total):
    b_idx = pl.program_id(0)
    p_idx = pl.program_id(1)
    n_pages = steps * pages
    rows = H_F * t_new
    half = pages // DECODE_GROUPS
    step_id = b_idx * steps + p_idx
    slot = step_id % DECODE_SLOTS

    def page_copies(step):
        b, p, slot = step // steps, step % steps, step % DECODE_SLOTS
        out = []
        for j in range(pages):
            page = pt_ref[b, n_pages - 1 - (p * pages + j)]
            out.append(pltpu.make_async_copy(k_hbm.at[page], k_buf.at[slot, j], sems.at[0, slot]))
            out.append(pltpu.make_async_copy(v_hbm.at[page], v_buf.at[slot, j], sems.at[1, slot]))
            out.append(pltpu.make_async_copy(lf_hbm.at[page], lf_buf.at[slot, j], sems.at[2, slot]))
        return out

    @pl.when(step_id == 0)
    def _():
        for ahead in range(min(DECODE_SLOTS - 1, total)):
            for c in page_copies(step_id + ahead):
                c.start()

    @pl.when(step_id + DECODE_SLOTS - 1 < total)
    def _():
        for c in page_copies(step_id + DECODE_SLOTS - 1):
            c.start()

    def expand(x):
        return jnp.broadcast_to(x[:, None, :], (H_F, t_new, x.shape[-1])).reshape(rows, x.shape[-1])

    @pl.when(p_idx == 0)
    def _():
        q = jnp.concatenate([q_ref[...].astype(F32)] * H_F, axis=0)
        r_head = lax.broadcasted_iota(jnp.int32, (rows, FQ), 0) // t_new
        l_head = lax.broadcasted_iota(jnp.int32, (rows, FQ), 1) // D_F
        qbd = jnp.where(r_head == l_head, q, 0.0).astype(BF16)
        qbd_scr[...] = qbd
        lfn = lfn_ref[0]
        lane = lax.broadcasted_iota(jnp.int32, lfn.shape, 1)
        cnew = jnp.zeros_like(lfn)
        for i in range(t_new):
            cnew = cnew + jnp.where(lane >= i, lfn[:, i:i + 1], 0.0)
        s = lax.dot_general(qbd, kn_ref[...].astype(BF16), _NT, preferred_element_type=F32) - expand(cnew)
        t_q = lax.broadcasted_iota(jnp.int32, s.shape, 0) % t_new
        t_k = lax.broadcasted_iota(jnp.int32, s.shape, 1)
        s = jnp.where(t_k <= t_q, s, NEG_BIG)
        m = jnp.max(s, axis=-1, keepdims=True)
        p = jnp.exp(s - m)
        m_scr[...] = jnp.full(m_scr.shape, NEG_BIG, F32)
        l_scr[...] = jnp.zeros(l_scr.shape, F32)
        acc_scr[...] = jnp.zeros(acc_scr.shape, F32)
        m_scr[0] = m
        l_scr[0] = jnp.sum(p, axis=-1, keepdims=True)
        acc_scr[0] = jnp.dot(p.astype(BF16), vn_ref[...].astype(BF16), preferred_element_type=F32)
        r_scr[...] = jnp.zeros(r_scr.shape, F32)

    for c in page_copies(step_id):
        c.wait()

    qbd = qbd_scr[...]
    run = r_scr[...]
    biases = []
    for j in range(pages):
        lf = lf_buf[slot, j]
        suffix = _lane_suffix_sum(lf)
        biases.append(expand(suffix - lf + run))
        run = run + suffix[:, 0:1]
    r_scr[...] = run
    for g in range(DECODE_GROUPS):
        js = range(g * half, (g + 1) * half)
        s = jnp.concatenate(
            [jnp.dot(qbd, k_buf[slot, j].reshape(FQ, LANES).astype(BF16), preferred_element_type=F32)
             + biases[j] for j in js], axis=-1)
        m_old = m_scr[g]
        m_new = jnp.maximum(m_old, jnp.max(s, axis=-1, keepdims=True))
        alpha = jnp.exp(m_old - m_new)
        p = jnp.exp(s - m_new)
        l_scr[g] = alpha * l_scr[g] + jnp.sum(p, axis=-1, keepdims=True)
        p = p.astype(BF16)
        acc = alpha * acc_scr[g]
        for i, j in enumerate(js):
            acc = acc + lax.dot_general(p[:, i * LANES:(i + 1) * LANES],
                                        v_buf[slot, j].reshape(FQ, LANES).astype(BF16), _NT,
                                        preferred_element_type=F32)
        acc_scr[g] = acc
        m_scr[g] = m_new

    @pl.when(p_idx == steps - 1)
    def _():
        m = jnp.max(m_scr[...], axis=0)
        scale = jnp.exp(m_scr[...] - m)
        full = jnp.sum(scale * acc_scr[...], axis=0) / jnp.sum(scale * l_scr[...], axis=0)
        l_head = lax.broadcasted_iota(jnp.int32, (t_new, FQ), 1) // D_F
        out = jnp.zeros((t_new, FQ), F32)
        for h in range(H_F):
            out = out + jnp.where(l_head == h, full[h * t_new:(h + 1) * t_new, :], 0.0)
        o_ref[...] = out.astype(o_ref.dtype)


def _fox_decode(fq, fk, fv, lf_new, cache_kt, cache_vt, cache_lft, page_table):
    db, n_pages = page_table.shape
    t_new = fq.shape[0] // db
    pages = math.gcd(PAGES_PER_STEP, n_pages)
    assert pages % DECODE_GROUPS == 0
    steps = n_pages // pages
    rows = H_F * t_new
    page_len = cache_kt.shape[-1]
    assert page_len == LANES
    tok = pl.BlockSpec((t_new, FQ), lambda b, p, pt: (b, 0))
    hbm = pl.BlockSpec(memory_space=pltpu.HBM)
    grid_spec = pltpu.PrefetchScalarGridSpec(
        num_scalar_prefetch=1,
        grid=(db, steps),
        in_specs=[tok, tok, tok, pl.BlockSpec((1, H_F, t_new), lambda b, p, pt: (b, 0, 0)), hbm, hbm, hbm],
        out_specs=tok,
        scratch_shapes=[pltpu.VMEM((DECODE_SLOTS, pages, H_F, D_F, page_len), F32),
                        pltpu.VMEM((DECODE_SLOTS, pages, H_F, D_F, page_len), F32),
                        pltpu.VMEM((DECODE_SLOTS, pages, H_F, page_len), F32),
                        pltpu.SemaphoreType.DMA((3, DECODE_SLOTS)),
                        pltpu.VMEM((rows, FQ), BF16),
                        pltpu.VMEM((DECODE_GROUPS, rows, 1), F32), pltpu.VMEM((DECODE_GROUPS, rows, 1), F32),
                        pltpu.VMEM((DECODE_GROUPS, rows, FQ), F32), pltpu.VMEM((H_F, 1), F32)],
    )
    return pl.pallas_call(
        functools.partial(_fox_decode_body, pages=pages, t_new=t_new, steps=steps, total=db * steps),
        grid_spec=grid_spec,
        out_shape=jax.ShapeDtypeStruct(fq.shape, fq.dtype),
        compiler_params=pltpu.CompilerParams(dimension_semantics=("arbitrary", "arbitrary")),
        name="fox_decode",
    )(page_table, fq, fk, fv, lf_new, cache_kt, cache_vt, cache_lft)


def _mem_attn_body(q_ref, k_ref, v_ref, o_ref):
    n_mem = k_ref.shape[1] // H_M
    outs = []
    for h in range(H_M):
        sl = slice(h * D_M, (h + 1) * D_M)
        head_rows = pl.ds(h, n_mem, stride=H_M)
        s = lax.dot_general(q_ref[:, sl].astype(BF16), k_ref[0, head_rows, :].astype(BF16), _NT,
                            preferred_element_type=F32) * (D_M ** -0.5)
        p = jnp.exp(s - jnp.max(s, axis=-1, keepdims=True))
        l = jnp.sum(p, axis=-1, keepdims=True)
        outs.append(jnp.dot(p.astype(BF16), v_ref[0, head_rows, :].astype(BF16), preferred_element_type=F32) / l)
    o_ref[...] = jnp.concatenate(outs, axis=-1).astype(o_ref.dtype)


def _mem_attn(mq, mem_k, mem_v):
    batch, mem_rows, _ = mem_k.shape
    length = mq.shape[0] // batch
    tq = min(ATTN_TILE, length)
    nt = length // tq
    row = pl.BlockSpec((tq, MQ), lambda b, i: (b * nt + i, 0))
    mem = pl.BlockSpec((1, mem_rows, D_M), lambda b, i: (b, 0, 0))
    return pl.pallas_call(_mem_attn_body, grid=(batch, nt), in_specs=[row, mem, mem], out_specs=row,
                          out_shape=jax.ShapeDtypeStruct(mq.shape, mq.dtype), name="mem_attn")(mq, mem_k, mem_v)


def _mem_kv_body(x_ref, w_ref, k_ref, v_ref):
    tm = x_ref.shape[0]
    kv = jnp.dot(x_ref[...].astype(BF16), w_ref[...], preferred_element_type=F32)
    for h in range(H_M):
        head_rows = pl.ds(h, tm, stride=H_M)
        k_ref[head_rows, :] = kv[:, h * D_M:(h + 1) * D_M]
        v_ref[head_rows, :] = kv[:, MQ + h * D_M:MQ + (h + 1) * D_M]


def _mem_kv(x, w):
    n, d = x.shape
    tm = min(TOKEN_TILE, n)
    out = pl.BlockSpec((tm * H_M, D_M), lambda i: (i, 0))
    shape = jax.ShapeDtypeStruct((n * H_M, D_M), F32)
    return pl.pallas_call(
        _mem_kv_body, grid=(n // tm,),
        in_specs=[pl.BlockSpec((tm, d), lambda i: (i, 0)), _resident(w.shape)],
        out_specs=[out, out], out_shape=[shape, shape], name="mem_kv_proj")(x, w)


def _merge_body(h_ref, ret_ref, fox_ref, mem_ref, gl_ref, w_ret, w_fox, w_mem, w_out, bg_ref, g_ref, b_ref,
                o_ref):
    d = h_ref.shape[1]
    branches = [jnp.dot(x_ref[...].astype(BF16), w_ref[...], preferred_element_type=F32)
                for x_ref, w_ref in ((ret_ref, w_ret), (fox_ref, w_fox), (mem_ref, w_mem))]
    mix = jnp.zeros(h_ref.shape, F32)
    for i, branch in enumerate(branches):
        mix = mix + jax.nn.sigmoid(gl_ref[:, i * d:(i + 1) * d] + bg_ref[i:i + 1, :]) * branch
    mixed = jnp.dot(mix.astype(BF16), w_out[...], preferred_element_type=F32)
    o_ref[...] = _layer_norm(ALPHA * h_ref[...] + mixed, g_ref[...], b_ref[...])


def _merge(h, ret, fox, mem, gl, w_ret, w_fox, w_mem, w_out, b_gate, g, b):
    n, d = h.shape
    tm = min(WIDE_TOKEN_TILE, n)
    row = lambda c: pl.BlockSpec((tm, c), lambda i: (i, 0))
    return pl.pallas_call(
        _merge_body, grid=(n // tm,),
        in_specs=[row(d), row(RV), row(FQ), row(MQ), row(N_BRANCH * d),
                  _resident(w_ret.shape), _resident(w_fox.shape), _resident(w_mem.shape), _resident(w_out.shape),
                  _resident(b_gate.shape), _resident((1, d)), _resident((1, d))],
        out_specs=row(d),
        out_shape=jax.ShapeDtypeStruct((n, d), F32), name="merge",
    )(h, ret, fox, mem, gl, w_ret, w_fox, w_mem, w_out, b_gate, g, b)


def _rope_tables(pos):
    half = DK_R // 2
    inv = ROPE_BASE ** (-jnp.arange(half, dtype=F32) / half)
    ang = pos[:, None] * inv[None, :]
    cos, sin = jnp.cos(ang), jnp.sin(ang)
    return (jnp.tile(jnp.concatenate([cos, cos], axis=1), (1, H_R)),
            jnp.tile(jnp.concatenate([-sin, sin], axis=1), (1, H_R)))


def _split_w_in(w_in, d_model, prefill):
    sizes = (RQ, RQ, RV, RV, FQ, FQ, FQ, H_F, MQ, N_BRANCH * d_model)
    names = ("rq", "rk", "rv", "rg", "fq", "fk", "fv", "fl", "mq", "gl")
    offs = np.concatenate([[0], np.cumsum(sizes)])
    w = {nm: w_in[:, int(offs[i]):int(offs[i + 1])].astype(BF16) for i, nm in enumerate(names)}
    if prefill:
        w["fl_nat"] = w["fl"]
        w["fq"], w["fk"], w["fv"] = w["fq"].T, w["fk"].T, w["fv"].T
    w["fl"] = w["fl"].T
    return w


def _layer(x, pos, groups, s0, mem_k, mem_v, fox_attend, prefill, p):
    d = x.shape[1]
    ln_g, ln_b = p["ln_g"], p["ln_b"]
    act_dtype = BF16 if (x.shape[0] // s0.shape[0]) % BF16_ROWS == 0 else F32
    h = _ffn_ln(x, p["wg"], p["wu"], p["wd"], 0, ln_g[0:1], ln_b[0:1])
    cos, sin = _rope_tables(pos)
    rq, rk, rv, rg, mq, gl, *fox_in = _in_proj(
        h, cos, sin, _split_w_in(p["w_in"], d, prefill), p["b_forget"], groups, prefill, act_dtype)
    ret, ret_state = _retention(rq, rk, rv, rg, s0, p["ret_gn_g"])
    fox = fox_attend(*fox_in)
    mem = _mem_attn(mq, mem_k, mem_v)
    h = _merge(h, ret, fox, mem, gl, p["w_ret_o"], p["w_fox_o"], p["w_mem_o"], p["w_out"], p["b_gate"],
               ln_g[1:2], ln_b[1:2])
    y = _ffn_ln(h, p["wg"], p["wu"], p["wd"], 1, ln_g[2:3], ln_b[2:3])
    return y, fox_in, ret_state


def kernel(x_prompt, x_sample, mem_prompt, cache_fox_k, cache_fox_v, cache_fox_logf, state_ret, cache_mem_k, cache_mem_v, page_table, ln_g, ln_b, w_ffn_gate, w_ffn_up, w_ffn_down, w_in, b_forget, b_gate, ret_gn_g, w_ret_o, w_fox_o, w_mem_k, w_mem_v, w_mem_o, w_out):
    bp, lp, d = x_prompt.shape
    db, ls, _ = x_sample.shape
    n_mem = mem_prompt.shape[1]
    n_pages = page_table.shape[1]
    page = cache_fox_k.shape[1]
    past_len = n_pages * page
    params = {
        "ln_g": ln_g, "ln_b": ln_b,
        "wg": w_ffn_gate.astype(BF16), "wu": w_ffn_up.astype(BF16), "wd": w_ffn_down.astype(BF16),
        "w_in": w_in, "b_forget": b_forget.reshape(H_F, 1), "b_gate": b_gate,
        "ret_gn_g": ret_gn_g.reshape(1, RV),
        "w_ret_o": w_ret_o.astype(BF16), "w_fox_o": w_fox_o.astype(BF16), "w_mem_o": w_mem_o.astype(BF16),
        "w_out": w_out.astype(BF16),
    }

    mem_k_p, mem_v_p = (t.reshape(bp, n_mem * H_M, D_M) for t in _mem_kv(
        mem_prompt.reshape(bp * n_mem, d), jnp.concatenate([w_mem_k, w_mem_v], axis=1).astype(BF16)))

    def attend_prompt(fqt, fk, fkt, fvt, fvt16, lft, lf):
        return _fox_prefill(fqt, fk, _forget_bias(lf, bp), fvt16)

    y_p, (_, _, fkt_p, fvt_p, _, lft_p, _), ret_p = _layer(
        x_prompt.reshape(bp * lp, d), jnp.arange(lp, dtype=F32), bp, jnp.zeros((bp, H_R, DK_R, DV_R), F32),
        mem_k_p, mem_v_p, attend_prompt, True, params)

    cache_kt = jnp.transpose(cache_fox_k, (0, 2, 3, 1))
    cache_vt = jnp.transpose(cache_fox_v, (0, 2, 3, 1))
    cache_lft = jnp.transpose(cache_fox_logf, (0, 2, 1))

    def attend_sample(fq, fk, fv, lft):
        lf_new = jnp.transpose(lft.reshape(H_F, db, ls), (1, 0, 2))
        return _fox_decode(fq, fk, fv, lf_new, cache_kt, cache_vt, cache_lft, page_table)

    y_s, (_, fk_s, fv_s, lft_s), ret_s = _layer(
        x_sample.reshape(db * ls, d), past_len + jnp.tile(jnp.arange(ls, dtype=F32), db), 1, state_ret,
        cache_mem_k.reshape(db, n_mem * H_M, D_M), cache_mem_v.reshape(db, n_mem * H_M, D_M), attend_sample,
        False, params)

    to_heads = lambda t, b, l: jnp.transpose(t.reshape(b, H_F, D_F, l), (0, 3, 1, 2))
    return (y_p.reshape(bp, lp, d), y_s.reshape(db, ls, d),
            to_heads(fkt_p, bp, lp), to_heads(fvt_p, bp, lp), jnp.transpose(lft_p, (0, 2, 1)),
            ret_p, mem_k_p.reshape(bp, n_mem, H_M, D_M), mem_v_p.reshape(bp, n_mem, H_M, D_M),
            fk_s.reshape(db, ls, H_F, D_F), fv_s.reshape(db, ls, H_F, D_F),
            jnp.transpose(lft_s.reshape(H_F, db, ls), (1, 2, 0)), ret_s)
```

```python
import functools
import math

import jax
import jax.numpy as jnp
import numpy as np
from jax import lax
from jax.experimental import pallas as pl
from jax.experimental.pallas import tpu as pltpu

F32 = jnp.float32
BF16 = jnp.bfloat16

DEPTH = 1
H_R, DK_R, DV_R = 4, 64, 128
RET_CHUNK = 128
RET_CHUNKS_PER_STEP = 4
SHORT_SEQS_PER_STEP = 8
ROPE_BASE = 10000.0
H_F, D_F = 8, 64
H_M, D_M = 4, 128
N_BRANCH = 3
LN_EPS = 1e-5
GN_EPS = 1e-5
ALPHA = (2.0 * DEPTH) ** 0.25
RQ, RV, FQ, MQ = H_R * DK_R, H_R * DV_R, H_F * D_F, H_M * D_M

LANES = 128
BF16_ROWS = 16
TOKEN_TILE = 512
WIDE_TOKEN_TILE = 512
FFN_SUB_ROWS = 256
ATTN_TILE = 512
PREFILL_LOOKAHEAD = 2
PREFILL_HEADS = 8
PAGES_PER_STEP = 16
DECODE_GROUPS = 2
DECODE_SLOTS = 3
NEG_BIG = -1e30
LOG2E = math.log2(math.e)
BIAS_PIECES = 3

_NT = (((1,), (1,)), ((), ()))
_TN = (((0,), (0,)), ((), ()))


def _resident(shape):
    return pl.BlockSpec(shape, lambda *_: (0,) * len(shape), pipeline_mode=pl.Buffered(1))


def _layer_norm(y, g, b):
    mu = jnp.mean(y, axis=-1, keepdims=True)
    yc = y - mu
    var = jnp.mean(yc * yc, axis=-1, keepdims=True)
    return yc * lax.rsqrt(var + LN_EPS) * g + b


def _silu(x):
    return x * jax.nn.sigmoid(x)


def _ffn_ln_body(x_ref, wg_ref, wu_ref, wd_ref, g_ref, b_ref, o_ref):
    n_sub = max(1, x_ref.shape[0] // FFN_SUB_ROWS)
    rows = [slice(i * (x_ref.shape[0] // n_sub), (i + 1) * (x_ref.shape[0] // n_sub)) for i in range(n_sub)]

    def gate_up(rs):
        xb = x_ref[rs, :].astype(BF16)
        return (jnp.dot(xb, wg_ref[...], preferred_element_type=F32),
                jnp.dot(xb, wu_ref[...], preferred_element_type=F32))

    def down(gu):
        return jnp.dot((_silu(gu[0]) * gu[1]).astype(BF16), wd_ref[...], preferred_element_type=F32)

    gu = gate_up(rows[0])
    f_prev = None
    for i, rs in enumerate(rows):
        gu_next = gate_up(rows[i + 1]) if i + 1 < n_sub else None
        f = down(gu)
        if f_prev is not None:
            o_ref[rows[i - 1], :] = _layer_norm(ALPHA * x_ref[rows[i - 1], :] + 0.5 * f_prev, g_ref[...], b_ref[...])
        gu, f_prev = gu_next, f
    o_ref[rows[-1], :] = _layer_norm(ALPHA * x_ref[rows[-1], :] + 0.5 * f_prev, g_ref[...], b_ref[...])


def _ffn_ln(x, wg, wu, wd, layer, g, b):
    n, d = x.shape
    d_ff = wg.shape[2]
    tm = min(WIDE_TOKEN_TILE, n)
    stacked = lambda r, c: pl.BlockSpec((None, r, c), lambda i: (layer, 0, 0), pipeline_mode=pl.Buffered(1))
    return pl.pallas_call(
        _ffn_ln_body,
        grid=(n // tm,),
        in_specs=[pl.BlockSpec((tm, d), lambda i: (i, 0)),
                  stacked(d, d_ff), stacked(d, d_ff), stacked(d_ff, d),
                  _resident((1, d)), _resident((1, d))],
        out_specs=pl.BlockSpec((tm, d), lambda i: (i, 0)),
        out_shape=jax.ShapeDtypeStruct((n, d), F32),
        name="ffn_ln",
    )(x, wg, wu, wd, g, b)


def _rotary(x, cos, sin_signed):
    width = x.shape[-1]
    half = DK_R // 2
    lane = lax.broadcasted_iota(jnp.int32, x.shape, 1)
    other = jnp.where((lane % DK_R) < half,
                      pltpu.roll(x, width - half, 1),
                      pltpu.roll(x, half, 1))
    return x * cos + other * sin_signed


def _log_sigmoid(x):
    return jnp.minimum(x, 0.0) - jnp.log1p(jnp.exp(-jnp.abs(x)))


def _in_proj_body(h_ref, cos_ref, sin_ref, w_rq, w_rk, w_rv, w_rg, w_mq, w_gl, w_fq, w_fk, w_fv, w_fl, bf_ref,
                  bg_ref, *rest, prefill):
    if prefill:
        w_fl_nat, bf_row_ref = rest[:2]
        rest = rest[2:]
    rq_o, rk_o, rv_o, rg_o, mq_o, gl_o = rest[:6]
    fox_o = rest[6:]
    hb = h_ref[...].astype(BF16)

    def proj(w):
        return jnp.dot(hb, w, preferred_element_type=F32)

    def proj_t(wt):
        return lax.dot_general(wt, hb, _NT, preferred_element_type=F32)

    cos, sin = cos_ref[...], sin_ref[...]
    rq_o[...] = _rotary(proj(w_rq[...]), cos, sin)
    rk_o[...] = _rotary(proj(w_rk[...]), cos, sin) * (DK_R ** -0.5)
    rv_o[...] = proj(w_rv[...]).astype(rv_o.dtype)
    rg_o[...] = proj(w_rg[...])
    mq_o[...] = proj(w_mq[...]).astype(mq_o.dtype)
    gl_o[...] = jax.nn.sigmoid(proj(w_gl[...]) + bg_ref[...]).astype(gl_o.dtype)
    if prefill:
        fqt_o, fk_o, fkt_o, fvt_o, fvt16_o, lft_o, lf_o = fox_o
        fqt_o[0] = (proj_t(w_fq[...]) * (D_F ** -0.5 * LOG2E)).astype(BF16)
        fk_o[...] = lax.dot_general(hb, w_fk[...], _NT, preferred_element_type=F32).astype(BF16)
        fkt_o[0] = proj_t(w_fk[...])
        fvt = proj_t(w_fv[...])
        fvt_o[0] = fvt
        fvt16_o[0] = fvt.astype(BF16)
        lft_o[0] = _log_sigmoid(proj_t(w_fl[...]) + bf_ref[...])
        lf_o[...] = _log_sigmoid(proj(w_fl_nat[...]) + bf_row_ref[...])
    else:
        fq_o, fk_o, fv_o, lft_o = fox_o
        fq_o[...] = (proj(w_fq[...]) * (D_F ** -0.5)).astype(fq_o.dtype)
        fk_o[...] = proj(w_fk[...])
        fv_o[...] = proj(w_fv[...])
        lft_o[0] = _log_sigmoid(proj_t(w_fl[...]) + bf_ref[...])


def _in_proj(h, cos, sin, w, b_forget, b_gate, groups, prefill, act_dtype):
    n, d = h.shape
    lg = n // groups
    tm = min(TOKEN_TILE, lg)
    nt = lg // tm
    row = lambda c: pl.BlockSpec((tm, c), lambda g, i: (g * nt + i, 0))
    tab = pl.BlockSpec((tm, RQ), lambda g, i: (i, 0))
    colt = lambda c: pl.BlockSpec((1, c, tm), lambda g, i: (g, 0, i))
    rows = lambda c, dt: jax.ShapeDtypeStruct((n, c), dt)
    cols = lambda c, dt: jax.ShapeDtypeStruct((groups, c, lg), dt)
    if prefill:
        fox_specs = [colt(FQ), row(FQ), colt(FQ), colt(FQ), colt(FQ), colt(H_F), row(H_F)]
        fox_shapes = [cols(FQ, BF16), rows(FQ, BF16), cols(FQ, F32), cols(FQ, F32), cols(FQ, BF16),
                      cols(H_F, F32), rows(H_F, F32)]
    else:
        fox_specs = [row(FQ), row(FQ), row(FQ), colt(H_F)]
        fox_shapes = [rows(FQ, act_dtype), rows(FQ, F32), rows(FQ, F32), cols(H_F, F32)]
    gates = w["gl"].shape[1]
    weights = [w[k] for k in ("rq", "rk", "rv", "rg", "mq", "gl", "fq", "fk", "fv", "fl")]
    weights += [b_forget, b_gate.reshape(1, gates)]
    if prefill:
        weights += [w["fl_nat"], b_forget.reshape(1, H_F)]
    return pl.pallas_call(
        functools.partial(_in_proj_body, prefill=prefill),
        grid=(groups, nt),
        in_specs=[row(d), tab, tab] + [_resident(a.shape) for a in weights],
        out_specs=[row(RQ), row(RQ), row(RV), row(RV), row(MQ), row(gates)] + fox_specs,
        out_shape=[rows(RQ, F32), rows(RQ, F32), rows(RV, act_dtype), rows(RV, F32), rows(MQ, act_dtype),
                   rows(gates, BF16)] + fox_shapes,
        name="in_proj",
    )(h, cos, sin, *weights)


def _retention_body(rq_ref, rk_ref, rv_ref, rg_ref, s0_ref, dmask_ref, qdec_ref, kdec_ref, gn_ref,
                    o_ref, sfin_ref, s_scr, *, chunk_decay):
    c = pl.program_id(1)

    @pl.when(c == 0)
    def _():
        s_scr[...] = s0_ref[...]

    chunk = dmask_ref.shape[1]
    seqs = s_scr.shape[0]
    seq_rows = rq_ref.shape[0] // seqs
    n_sub = seq_rows // chunk
    ks = lambda h: slice(h * DK_R, (h + 1) * DK_R)
    vs = lambda h: slice(h * DV_R, (h + 1) * DV_R)
    rs = lambda sq, i: slice(sq * seq_rows + i * chunk, sq * seq_rows + (i + 1) * chunk)
    units = [(sq, h, i) for sq in range(seqs) for h in range(H_R) for i in range(n_sub)]
    decayed = {u: lax.dot_general(rq_ref[rs(u[0], u[2]), ks(u[1])].astype(BF16),
                                  rk_ref[rs(u[0], u[2]), ks(u[1])].astype(BF16), _NT,
                                  preferred_element_type=F32) * dmask_ref[u[1]] for u in units}
    intra, kv = {}, {}
    for sq, h, i in units:
        vh = rv_ref[rs(sq, i), vs(h)].astype(BF16)
        intra[sq, h, i] = jnp.dot(decayed[sq, h, i].astype(BF16), vh, preferred_element_type=F32)
        kv[sq, h, i] = lax.dot_general((rk_ref[rs(sq, i), ks(h)] * kdec_ref[:, ks(h)]).astype(BF16), vh, _TN,
                                       preferred_element_type=F32)
    for sq, h in [(sq, h) for sq in range(seqs) for h in range(H_R)]:
        s_h = s_scr[sq, h]
        for i in range(n_sub):
            o = intra[sq, h, i] + jnp.dot((rq_ref[rs(sq, i), ks(h)] * qdec_ref[:, ks(h)]).astype(BF16),
                                          s_h.astype(BF16), preferred_element_type=F32)
            s_h = s_h * chunk_decay[h] + kv[sq, h, i]
            mu = jnp.mean(o, axis=-1, keepdims=True)
            oc = o - mu
            var = jnp.mean(oc * oc, axis=-1, keepdims=True)
            o_ref[rs(sq, i), vs(h)] = (_silu(rg_ref[rs(sq, i), vs(h)])
                                       * (oc * lax.rsqrt(var + GN_EPS) * gn_ref[:, vs(h)])).astype(o_ref.dtype)
        s_scr[sq, h] = s_h

    @pl.when(c == pl.num_programs(1) - 1)
    def _():
        sfin_ref[...] = s_scr[...]


def _retention_tables(chunk):
    log_g = jnp.log1p(-jnp.exp2(-5.0 - jnp.arange(H_R, dtype=F32)))
    i = jnp.arange(chunk, dtype=F32)
    diff = i[:, None] - i[None, :]
    dmask = jnp.where(diff[None] >= 0, jnp.exp(jnp.maximum(diff, 0.0)[None] * log_g[:, None, None]), 0.0)
    q_dec = jnp.exp((i + 1.0)[:, None] * log_g[None, :])
    k_dec = jnp.exp((chunk - 1.0 - i)[:, None] * log_g[None, :])
    widen = lambda t: jnp.repeat(t, DK_R, axis=1)
    return dmask, widen(q_dec), widen(k_dec)


def _chunk_decay(chunk):
    log_g = np.log1p(-np.exp2(-5.0 - np.arange(H_R, dtype=np.float32))).astype(np.float32)
    return tuple(float(v) for v in np.exp(np.float32(chunk) * log_g).astype(np.float32))


def _retention(rq, rk, rv, rg, s0, gn_g):
    batch = s0.shape[0]
    n = rq.shape[0]
    length = n // batch
    chunk = math.gcd(length, RET_CHUNK)
    rows = chunk * math.gcd(length // chunk, RET_CHUNKS_PER_STEP)
    nc = length // rows
    short = nc == 1 and length * SHORT_SEQS_PER_STEP <= RET_CHUNK * RET_CHUNKS_PER_STEP
    seqs = math.gcd(batch, SHORT_SEQS_PER_STEP) if short else 1
    dmask, q_dec, k_dec = _retention_tables(chunk)
    row = lambda c: pl.BlockSpec((seqs * rows, c), lambda b, i: (b * nc + i, 0))
    state = pl.BlockSpec((seqs, H_R, DK_R, DV_R), lambda b, i: (b, 0, 0, 0))
    return pl.pallas_call(
        functools.partial(_retention_body, chunk_decay=_chunk_decay(chunk)),
        grid=(batch // seqs, nc),
        in_specs=[row(RQ), row(RQ), row(RV), row(RV), state,
                  _resident((H_R, chunk, chunk)), _resident((chunk, RQ)), _resident((chunk, RQ)),
                  _resident((1, RV))],
        out_specs=[row(RV), state],
        out_shape=[jax.ShapeDtypeStruct((n, RV), rv.dtype),
                   jax.ShapeDtypeStruct((batch, H_R, DK_R, DV_R), F32)],
        scratch_shapes=[pltpu.VMEM((seqs, H_R, DK_R, DV_R), F32)],
        name="retention",
    )(rq, rk, rv, rg, s0, dmask, q_dec, k_dec, gn_g)


def _lane_suffix_sum(x):
    lane = lax.broadcasted_iota(jnp.int32, x.shape, 1)
    d = 1
    while d < LANES:
        x = x + jnp.where(lane < LANES - d, pltpu.roll(x, LANES - d, 1), 0.0)
        d *= 2
    return x


def _bf16_pieces(x):
    pieces = []
    for _ in range(BIAS_PIECES):
        piece = x.astype(BF16)
        pieces.append(piece)
        x = x - piece.astype(F32)
    return pieces


def _forget_bias_body(lf_ref, tri_ref, place_ref, o_ref, carry_scr):
    @pl.when(pl.program_id(1) == 0)
    def _():
        carry_scr[...] = jnp.zeros(carry_scr.shape, F32)

    rows = lf_ref.shape[0]
    c = carry_scr[...] + sum(jnp.dot(tri_ref[...], piece, preferred_element_type=F32)
                             for piece in _bf16_pieces(lf_ref[...]))
    carry_scr[...] = c[rows - 1:rows, :]
    planes = sum(jnp.dot(piece, place_ref[i], preferred_element_type=F32)
                 for i, piece in enumerate(_bf16_pieces(c * (-LOG2E))))
    for j in range(H_F // 2):
        o_ref[0, j] = planes[:, j * 2 * D_F:(j + 1) * 2 * D_F].astype(BF16)


def _bias_placement():
    plane_w = 2 * D_F
    place = np.zeros((BIAS_PIECES, H_F, (H_F // 2) * plane_w), np.float32)
    for i in range(BIAS_PIECES):
        for j in range(H_F // 2):
            place[i, 2 * j, j * plane_w + D_F + i] = 1.0
            place[i, 2 * j + 1, j * plane_w + i] = 1.0
    return jnp.asarray(place, BF16)


def _forget_bias(lf, batch):
    length = lf.shape[0] // batch
    rows = min(ATTN_TILE, length)
    nt = length // rows
    tri = jnp.asarray(np.tril(np.ones((rows, rows), np.float32)), BF16)
    place = _bias_placement()
    return pl.pallas_call(
        _forget_bias_body, grid=(batch, nt),
        in_specs=[pl.BlockSpec((rows, H_F), lambda b, i: (b * nt + i, 0)), _resident(tri.shape),
                  _resident(place.shape)],
        out_specs=pl.BlockSpec((1, H_F // 2, rows, 2 * D_F), lambda b, i: (b, 0, i, 0)),
        out_shape=jax.ShapeDtypeStruct((batch, H_F // 2, length, 2 * D_F), BF16),
        scratch_shapes=[pltpu.VMEM((1, H_F), F32)],
        name="forget_bias",
    )(lf, tri, place)


def _fox_prefill_body(qi_tab, ki_tab, qt_ref, k_ref, kb_ref, vt_ref, o_ref, qa_scr, m_scr, acc_scr, *, heads):
    step_id = pl.program_id(2)
    qi = qi_tab[step_id]
    ki = ki_tab[step_id]
    pair_w = 2 * D_F
    acc_rows = D_F + BF16_ROWS

    @pl.when(ki == 0)
    def _():
        m_scr[...] = jnp.full(m_scr.shape, NEG_BIG, F32)
        acc_scr[...] = jnp.zeros(acc_scr.shape, F32)
        row = lax.broadcasted_iota(jnp.int32, (pair_w, qt_ref.shape[2]), 0)
        for h in range(heads):
            pair = qt_ref[0, (h // 2) * pair_w:(h // 2 + 1) * pair_w, :]
            first = D_F if h % 2 == 0 else 0
            qa_scr[h] = jnp.where((row >= first) & (row < first + BIAS_PIECES), jnp.ones_like(pair), pair)

    def step(masked):
        lane = lax.broadcasted_iota(jnp.int32, (k_ref.shape[0], pair_w), 1)
        ones = jnp.ones((BF16_ROWS, k_ref.shape[0]), BF16)

        def logits(h):
            j = h // 2
            own = (lane < D_F) if h % 2 == 0 else (lane >= D_F)
            ka = jnp.where(own, k_ref[:, j * pair_w:(j + 1) * pair_w], kb_ref[0, j])
            return jnp.dot(ka, qa_scr[h], preferred_element_type=F32)

        ahead = [logits(h) for h in range(min(PREFILL_LOOKAHEAD, heads))]
        for h in range(heads):
            st = ahead.pop(0)
            if h + PREFILL_LOOKAHEAD < heads:
                ahead.append(logits(h + PREFILL_LOOKAHEAD))
            if masked:
                key = lax.broadcasted_iota(jnp.int32, st.shape, 0)
                qry = lax.broadcasted_iota(jnp.int32, st.shape, 1)
                st = jnp.where(key <= qry, st, NEG_BIG)
            m_old = m_scr[h]
            m_new = jnp.maximum(m_old, jnp.max(st, axis=0, keepdims=True))
            alpha = jnp.exp2(m_old - m_new)
            pt = jnp.exp2(st - m_new)
            va = jnp.concatenate([vt_ref[0, h * D_F:(h + 1) * D_F, :], ones], axis=0)
            rows = slice(h * acc_rows, (h + 1) * acc_rows)
            acc_scr[rows, :] = alpha * acc_scr[rows, :] + jnp.dot(va, pt.astype(BF16),
                                                                  preferred_element_type=F32)
            m_scr[h] = m_new

    @pl.when(ki < qi)
    def _():
        step(False)

    @pl.when(ki == qi)
    def _():
        step(True)
        out_t = jnp.concatenate([acc_scr[h * acc_rows:h * acc_rows + D_F, :]
                                 / acc_scr[h * acc_rows + D_F:h * acc_rows + D_F + 1, :] for h in range(heads)],
                                axis=0)
        o_ref[...] = out_t.T.astype(BF16)


def _fox_prefill(fqt, fk, kbias, fvt):
    batch, _, length = fqt.shape
    tile = min(ATTN_TILE, length)
    nt = length // tile
    heads = PREFILL_HEADS
    width = heads * D_F
    pairs = [(q, k) for q in range(nt) for k in range(q + 1)]
    qi_tab = jnp.asarray([q for q, _ in pairs], jnp.int32)
    ki_tab = jnp.asarray([k for _, k in pairs], jnp.int32)
    grid_spec = pltpu.PrefetchScalarGridSpec(
        num_scalar_prefetch=2,
        grid=(batch, H_F // heads, len(pairs)),
        in_specs=[pl.BlockSpec((1, width, tile), lambda b, g, p, qt, kt: (b, g, qt[p])),
                  pl.BlockSpec((tile, width), lambda b, g, p, qt, kt: (b * nt + kt[p], g)),
                  pl.BlockSpec((1, heads // 2, tile, 2 * D_F), lambda b, g, p, qt, kt: (b, g, kt[p], 0)),
                  pl.BlockSpec((1, width, tile), lambda b, g, p, qt, kt: (b, g, kt[p]))],
        out_specs=pl.BlockSpec((tile, width), lambda b, g, p, qt, kt: (b * nt + qt[p], g)),
        scratch_shapes=[pltpu.VMEM((heads, 2 * D_F, tile), BF16), pltpu.VMEM((heads, 1, tile), F32),
                        pltpu.VMEM((heads * (D_F + BF16_ROWS), tile), F32)],
    )
    return pl.pallas_call(
        functools.partial(_fox_prefill_body, heads=heads),
        grid_spec=grid_spec,
        out_shape=jax.ShapeDtypeStruct(fk.shape, BF16),
        name="fox_prefill",
    )(qi_tab, ki_tab, fqt, fk, kbias, fvt)


def _fox_decode_body(pt_ref, q_ref, kn_ref, vn_ref, lfn_ref, k_hbm, v_hbm, lf_hbm, o_ref,
                     k_buf, v_buf, lf_buf, sems, qbd_scr, m_scr, l_scr, acc_scr, r_scr, *, pages, t_new, steps,
                     total):
    b_idx = pl.program_id(0)
    p_idx = pl.program_id(1)
    n_pages = steps * pages
    rows = H_F * t_new
    half = pages // DECODE_GROUPS
    step_id = b_idx * steps + p_idx
    slot = step_id % DECODE_SLOTS

    def page_copies(step):
        b, p, slot = step // steps, step % steps, step % DECODE_SLOTS
        out = []
        for j in range(pages):
            page = pt_ref[b, n_pages - 1 - (p * pages + j)]
            out.append(pltpu.make_async_copy(k_hbm.at[page], k_buf.at[slot, j], sems.at[0, slot]))
            out.append(pltpu.make_async_copy(v_hbm.at[page], v_buf.at[slot, j], sems.at[1, slot]))
            out.append(pltpu.make_async_copy(lf_hbm.at[page], lf_buf.at[slot, j], sems.at[2, slot]))
        return out

    @pl.when(step_id == 0)
    def _():
        for ahead in range(min(DECODE_SLOTS - 1, total)):
            for c in page_copies(step_id + ahead):
                c.start()

    @pl.when(step_id + DECODE_SLOTS - 1 < total)
    def _():
        for c in page_copies(step_id + DECODE_SLOTS - 1):
            c.start()

    def expand(x):
        return jnp.broadcast_to(x[:, None, :], (H_F, t_new, x.shape[-1])).reshape(rows, x.shape[-1])

    @pl.when(p_idx == 0)
    def _():
        q = jnp.concatenate([q_ref[...].astype(F32)] * H_F, axis=0)
        r_head = lax.broadcasted_iota(jnp.int32, (rows, FQ), 0) // t_new
        l_head = lax.broadcasted_iota(jnp.int32, (rows, FQ), 1) // D_F
        qbd = jnp.where(r_head == l_head, q, 0.0).astype(BF16)
        qbd_scr[...] = qbd
        lfn = lfn_ref[0]
        lane = lax.broadcasted_iota(jnp.int32, lfn.shape, 1)
        cnew = jnp.zeros_like(lfn)
        for i in range(t_new):
            cnew = cnew + jnp.where(lane >= i, lfn[:, i:i + 1], 0.0)
        s = lax.dot_general(qbd, kn_ref[...].astype(BF16), _NT, preferred_element_type=F32) - expand(cnew)
        t_q = lax.broadcasted_iota(jnp.int32, s.shape, 0) % t_new
        t_k = lax.broadcasted_iota(jnp.int32, s.shape, 1)
        s = jnp.where(t_k <= t_q, s, NEG_BIG)
        m = jnp.max(s, axis=-1, keepdims=True)
        p = jnp.exp(s - m)
        m_scr[...] = jnp.full(m_scr.shape, NEG_BIG, F32)
        l_scr[...] = jnp.zeros(l_scr.shape, F32)
        acc_scr[...] = jnp.zeros(acc_scr.shape, F32)
        m_scr[0] = m
        l_scr[0] = jnp.sum(p, axis=-1, keepdims=True)
        acc_scr[0] = jnp.dot(p.astype(BF16), vn_ref[...].astype(BF16), preferred_element_type=F32)
        r_scr[...] = jnp.zeros(r_scr.shape, F32)

    for c in page_copies(step_id):
        c.wait()

    qbd = qbd_scr[...]
    run = r_scr[...]
    biases = []
    for j in range(pages):
        lf = lf_buf[slot, j]
        suffix = _lane_suffix_sum(lf)
        biases.append(expand(suffix - lf + run))
        run = run + suffix[:, 0:1]
    r_scr[...] = run
    for g in range(DECODE_GROUPS):
        js = range(g * half, (g + 1) * half)
        s = jnp.concatenate(
            [jnp.dot(qbd, k_buf[slot, j].reshape(FQ, LANES).astype(BF16), preferred_element_type=F32)
             + biases[j] for j in js], axis=-1)
        m_old = m_scr[g]
        m_new = jnp.maximum(m_old, jnp.max(s, axis=-1, keepdims=True))
        alpha = jnp.exp(m_old - m_new)
        p = jnp.exp(s - m_new)
        l_scr[g] = alpha * l_scr[g] + jnp.sum(p, axis=-1, keepdims=True)
        p = p.astype(BF16)
        acc = alpha * acc_scr[g]
        for i, j in enumerate(js):
            acc = acc + lax.dot_general(p[:, i * LANES:(i + 1) * LANES],
                                        v_buf[slot, j].reshape(FQ, LANES).astype(BF16), _NT,
                                        preferred_element_type=F32)
        acc_scr[g] = acc
        m_scr[g] = m_new

    @pl.when(p_idx == steps - 1)
    def _():
        m = jnp.max(m_scr[...], axis=0)
        scale = jnp.exp(m_scr[...] - m)
        full = jnp.sum(scale * acc_scr[...], axis=0) / jnp.sum(scale * l_scr[...], axis=0)
        l_head = lax.broadcasted_iota(jnp.int32, (t_new, FQ), 1) // D_F
        out = jnp.zeros((t_new, FQ), F32)
        for h in range(H_F):
            out = out + jnp.where(l_head == h, full[h * t_new:(h + 1) * t_new, :], 0.0)
        o_ref[...] = out.astype(o_ref.dtype)


def _fox_decode(fq, fk, fv, lf_new, cache_kt, cache_vt, cache_lft, page_table):
    db, n_pages = page_table.shape
    t_new = fq.shape[0] // db
    pages = math.gcd(PAGES_PER_STEP, n_pages)
    assert pages % DECODE_GROUPS == 0
    steps = n_pages // pages
    rows = H_F * t_new
    page_len = cache_kt.shape[-1]
    assert page_len == LANES
    tok = pl.BlockSpec((t_new, FQ), lambda b, p, pt: (b, 0))
    hbm = pl.BlockSpec(memory_space=pltpu.HBM)
    grid_spec = pltpu.PrefetchScalarGridSpec(
        num_scalar_prefetch=1,
        grid=(db, steps),
        in_specs=[tok, tok, tok, pl.BlockSpec((1, H_F, t_new), lambda b, p, pt: (b, 0, 0)), hbm, hbm, hbm],
        out_specs=tok,
        scratch_shapes=[pltpu.VMEM((DECODE_SLOTS, pages, H_F, D_F, page_len), F32),
                        pltpu.VMEM((DECODE_SLOTS, pages, H_F, D_F, page_len), F32),
                        pltpu.VMEM((DECODE_SLOTS, pages, H_F, page_len), F32),
                        pltpu.SemaphoreType.DMA((3, DECODE_SLOTS)),
                        pltpu.VMEM((rows, FQ), BF16),
                        pltpu.VMEM((DECODE_GROUPS, rows, 1), F32), pltpu.VMEM((DECODE_GROUPS, rows, 1), F32),
                        pltpu.VMEM((DECODE_GROUPS, rows, FQ), F32), pltpu.VMEM((H_F, 1), F32)],
    )
    return pl.pallas_call(
        functools.partial(_fox_decode_body, pages=pages, t_new=t_new, steps=steps, total=db * steps),
        grid_spec=grid_spec,
        out_shape=jax.ShapeDtypeStruct(fq.shape, fq.dtype),
        compiler_params=pltpu.CompilerParams(dimension_semantics=("arbitrary", "arbitrary")),
        name="fox_decode",
    )(page_table, fq, fk, fv, lf_new, cache_kt, cache_vt, cache_lft)


def _mem_attn_body(q_ref, k_ref, v_ref, o_ref):
    seqs = k_ref.shape[0]
    tq = q_ref.shape[0] // seqs
    n_mem = k_ref.shape[1] // H_M
    units = [(sq, h) for sq in range(seqs) for h in range(H_M)]
    rows = lambda sq: slice(sq * tq, (sq + 1) * tq)
    lanes = lambda h: slice(h * D_M, (h + 1) * D_M)
    head_rows = lambda h: pl.ds(h, n_mem, stride=H_M)
    scores = [lax.dot_general(q_ref[rows(sq), lanes(h)].astype(BF16), k_ref[sq, head_rows(h), :].astype(BF16),
                              _NT, preferred_element_type=F32) * (D_M ** -0.5) for sq, h in units]
    for (sq, h), s in zip(units, scores):
        p = jnp.exp(s - jnp.max(s, axis=-1, keepdims=True))
        l = jnp.sum(p, axis=-1, keepdims=True)
        o_ref[rows(sq), lanes(h)] = (jnp.dot(p.astype(BF16), v_ref[sq, head_rows(h), :].astype(BF16),
                                             preferred_element_type=F32) / l).astype(o_ref.dtype)


def _mem_attn(mq, mem_k, mem_v):
    batch, mem_rows, _ = mem_k.shape
    length = mq.shape[0] // batch
    tq = min(ATTN_TILE, length)
    nt = length // tq
    seqs = math.gcd(batch, SHORT_SEQS_PER_STEP) if nt == 1 and length * SHORT_SEQS_PER_STEP <= ATTN_TILE else 1
    row = pl.BlockSpec((seqs * tq, MQ), lambda b, i: (b * nt + i, 0))
    mem = pl.BlockSpec((seqs, mem_rows, D_M), lambda b, i: (b, 0, 0))
    return pl.pallas_call(_mem_attn_body, grid=(batch // seqs, nt), in_specs=[row, mem, mem], out_specs=row,
                          out_shape=jax.ShapeDtypeStruct(mq.shape, mq.dtype), name="mem_attn")(mq, mem_k, mem_v)


def _mem_kv_body(x_ref, w_ref, k_ref, v_ref):
    tm = x_ref.shape[0]
    kv = jnp.dot(x_ref[...].astype(BF16), w_ref[...], preferred_element_type=F32)
    for h in range(H_M):
        head_rows = pl.ds(h, tm, stride=H_M)
        k_ref[head_rows, :] = kv[:, h * D_M:(h + 1) * D_M]
        v_ref[head_rows, :] = kv[:, MQ + h * D_M:MQ + (h + 1) * D_M]


def _mem_kv(x, w):
    n, d = x.shape
    tm = min(TOKEN_TILE, n)
    out = pl.BlockSpec((tm * H_M, D_M), lambda i: (i, 0))
    shape = jax.ShapeDtypeStruct((n * H_M, D_M), F32)
    return pl.pallas_call(
        _mem_kv_body, grid=(n // tm,),
        in_specs=[pl.BlockSpec((tm, d), lambda i: (i, 0)), _resident(w.shape)],
        out_specs=[out, out], out_shape=[shape, shape], name="mem_kv_proj")(x, w)


def _merge_body(h_ref, ret_ref, fox_ref, mem_ref, gate_ref, w_ret, w_fox, w_mem, w_out, g_ref, b_ref, o_ref):
    d = h_ref.shape[1]
    branches = [jnp.dot(x_ref[...].astype(BF16), w_ref[...], preferred_element_type=F32)
                for x_ref, w_ref in ((ret_ref, w_ret), (fox_ref, w_fox), (mem_ref, w_mem))]
    mix = jnp.zeros(h_ref.shape, F32)
    for i, branch in enumerate(branches):
        mix = mix + gate_ref[:, i * d:(i + 1) * d].astype(F32) * branch
    mixed = jnp.dot(mix.astype(BF16), w_out[...], preferred_element_type=F32)
    o_ref[...] = _layer_norm(ALPHA * h_ref[...] + mixed, g_ref[...], b_ref[...])


def _merge(h, ret, fox, mem, gate, w_ret, w_fox, w_mem, w_out, g, b):
    n, d = h.shape
    tm = min(WIDE_TOKEN_TILE, n)
    row = lambda c: pl.BlockSpec((tm, c), lambda i: (i, 0))
    return pl.pallas_call(
        _merge_body, grid=(n // tm,),
        in_specs=[row(d), row(RV), row(FQ), row(MQ), row(N_BRANCH * d),
                  _resident(w_ret.shape), _resident(w_fox.shape), _resident(w_mem.shape), _resident(w_out.shape),
                  _resident((1, d)), _resident((1, d))],
        out_specs=row(d),
        out_shape=jax.ShapeDtypeStruct((n, d), F32), name="merge",
    )(h, ret, fox, mem, gate, w_ret, w_fox, w_mem, w_out, g, b)


def _rope_tables(pos):
    half = DK_R // 2
    inv = ROPE_BASE ** (-jnp.arange(half, dtype=F32) / half)
    ang = pos[:, None] * inv[None, :]
    cos, sin = jnp.cos(ang), jnp.sin(ang)
    return (jnp.tile(jnp.concatenate([cos, cos], axis=1), (1, H_R)),
            jnp.tile(jnp.concatenate([-sin, sin], axis=1), (1, H_R)))


def _split_w_in(w_in, d_model, prefill):
    sizes = (RQ, RQ, RV, RV, FQ, FQ, FQ, H_F, MQ, N_BRANCH * d_model)
    names = ("rq", "rk", "rv", "rg", "fq", "fk", "fv", "fl", "mq", "gl")
    offs = np.concatenate([[0], np.cumsum(sizes)])
    w = {nm: w_in[:, int(offs[i]):int(offs[i + 1])].astype(BF16) for i, nm in enumerate(names)}
    if prefill:
        w["fl_nat"] = w["fl"]
        w["fq"], w["fk"], w["fv"] = w["fq"].T, w["fk"].T, w["fv"].T
    w["fl"] = w["fl"].T
    return w


def _layer(x, pos, groups, s0, mem_k, mem_v, fox_attend, prefill, p):
    d = x.shape[1]
    ln_g, ln_b = p["ln_g"], p["ln_b"]
    act_dtype = BF16 if (x.shape[0] // s0.shape[0]) % BF16_ROWS == 0 else F32
    h = _ffn_ln(x, p["wg"], p["wu"], p["wd"], 0, ln_g[0:1], ln_b[0:1])
    cos, sin = _rope_tables(pos)
    rq, rk, rv, rg, mq, gl, *fox_in = _in_proj(
        h, cos, sin, _split_w_in(p["w_in"], d, prefill), p["b_forget"], p["b_gate"], groups, prefill, act_dtype)
    ret, ret_state = _retention(rq, rk, rv, rg, s0, p["ret_gn_g"])
    fox = fox_attend(*fox_in)
    mem = _mem_attn(mq, mem_k, mem_v)
    h = _merge(h, ret, fox, mem, gl, p["w_ret_o"], p["w_fox_o"], p["w_mem_o"], p["w_out"], ln_g[1:2], ln_b[1:2])
    y = _ffn_ln(h, p["wg"], p["wu"], p["wd"], 1, ln_g[2:3], ln_b[2:3])
    return y, fox_in, ret_state


def kernel(x_prompt, x_sample, mem_prompt, cache_fox_k, cache_fox_v, cache_fox_logf, state_ret, cache_mem_k, cache_mem_v, page_table, ln_g, ln_b, w_ffn_gate, w_ffn_up, w_ffn_down, w_in, b_forget, b_gate, ret_gn_g, w_ret_o, w_fox_o, w_mem_k, w_mem_v, w_mem_o, w_out):
    bp, lp, d = x_prompt.shape
    db, ls, _ = x_sample.shape
    n_mem = mem_prompt.shape[1]
    n_pages = page_table.shape[1]
    page = cache_fox_k.shape[1]
    past_len = n_pages * page
    params = {
        "ln_g": ln_g, "ln_b": ln_b,
        "wg": w_ffn_gate.astype(BF16), "wu": w_ffn_up.astype(BF16), "wd": w_ffn_down.astype(BF16),
        "w_in": w_in, "b_forget": b_forget.reshape(H_F, 1), "b_gate": b_gate,
        "ret_gn_g": ret_gn_g.reshape(1, RV),
        "w_ret_o": w_ret_o.astype(BF16), "w_fox_o": w_fox_o.astype(BF16), "w_mem_o": w_mem_o.astype(BF16),
        "w_out": w_out.astype(BF16),
    }

    mem_k_p, mem_v_p = (t.reshape(bp, n_mem * H_M, D_M) for t in _mem_kv(
        mem_prompt.reshape(bp * n_mem, d), jnp.concatenate([w_mem_k, w_mem_v], axis=1).astype(BF16)))

    def attend_prompt(fqt, fk, fkt, fvt, fvt16, lft, lf):
        return _fox_prefill(fqt, fk, _forget_bias(lf, bp), fvt16)

    y_p, (_, _, fkt_p, fvt_p, _, lft_p, _), ret_p = _layer(
        x_prompt.reshape(bp * lp, d), jnp.arange(lp, dtype=F32), bp, jnp.zeros((bp, H_R, DK_R, DV_R), F32),
        mem_k_p, mem_v_p, attend_prompt, True, params)

    cache_kt = jnp.transpose(cache_fox_k, (0, 2, 3, 1))
    cache_vt = jnp.transpose(cache_fox_v, (0, 2, 3, 1))
    cache_lft = jnp.transpose(cache_fox_logf, (0, 2, 1))

    def attend_sample(fq, fk, fv, lft):
        lf_new = jnp.transpose(lft.reshape(H_F, db, ls), (1, 0, 2))
        return _fox_decode(fq, fk, fv, lf_new, cache_kt, cache_vt, cache_lft, page_table)

    y_s, (_, fk_s, fv_s, lft_s), ret_s = _layer(
        x_sample.reshape(db * ls, d), past_len + jnp.tile(jnp.arange(ls, dtype=F32), db), 1, state_ret,
        cache_mem_k.reshape(db, n_mem * H_M, D_M), cache_mem_v.reshape(db, n_mem * H_M, D_M), attend_sample,
        False, params)

    to_heads = lambda t, b, l: jnp.transpose(t.reshape(b, H_F, D_F, l), (0, 3, 1, 2))
    return (y_p.reshape(bp, lp, d), y_s.reshape(db, ls, d),
            to_heads(fkt_p, bp, lp), to_heads(fvt_p, bp, lp), jnp.transpose(lft_p, (0, 2, 1)),
            ret_p, mem_k_p.reshape(bp, n_mem, H_M, D_M), mem_v_p.reshape(bp, n_mem, H_M, D_M),
            fk_s.reshape(db, ls, H_F, D_F), fv_s.reshape(db, ls, H_F, D_F),
            jnp.transpose(lft_s.reshape(H_F, db, ls), (1, 2, 0)), ret_s)
```

```python
import functools
import math

import jax
import jax.numpy as jnp
import numpy as np
from jax import lax
from jax.experimental import pallas as pl
from jax.experimental.pallas import tpu as pltpu

F32 = jnp.float32
BF16 = jnp.bfloat16

DEPTH = 1
H_R, DK_R, DV_R = 4, 64, 128
RET_CHUNK = 128
RET_CHUNKS_PER_STEP = 4
SHORT_SEQS_PER_STEP = 8
ROPE_BASE = 10000.0
H_F, D_F = 8, 64
H_M, D_M = 4, 128
N_BRANCH = 3
LN_EPS = 1e-5
GN_EPS = 1e-5
ALPHA = (2.0 * DEPTH) ** 0.25
RQ, RV, FQ, MQ = H_R * DK_R, H_R * DV_R, H_F * D_F, H_M * D_M

LANES = 128
BF16_ROWS = 16
TOKEN_TILE = 256
WIDE_TOKEN_TILE = 512
FFN_SUB_ROWS = 256
ATTN_TILE = 512
PREFILL_LOOKAHEAD = 2
PREFILL_HEADS = 8
PAGES_PER_STEP = 16
DECODE_GROUPS = 2
DECODE_SLOTS = 3
NEG_BIG = -1e30
LOG2E = math.log2(math.e)
BIAS_PIECES = 3

_NT = (((1,), (1,)), ((), ()))
_TN = (((0,), (0,)), ((), ()))


def _resident(shape):
    return pl.BlockSpec(shape, lambda *_: (0,) * len(shape), pipeline_mode=pl.Buffered(1))


def _layer_norm(y, g, b):
    mu = jnp.mean(y, axis=-1, keepdims=True)
    yc = y - mu
    var = jnp.mean(yc * yc, axis=-1, keepdims=True)
    return yc * lax.rsqrt(var + LN_EPS) * g + b


def _silu(x):
    return x * jax.nn.sigmoid(x)


def _ffn_ln_body(x_ref, wg_ref, wu_ref, wd_ref, g_ref, b_ref, o_ref):
    n_sub = max(1, x_ref.shape[0] // FFN_SUB_ROWS)
    rows = [slice(i * (x_ref.shape[0] // n_sub), (i + 1) * (x_ref.shape[0] // n_sub)) for i in range(n_sub)]

    def gate_up(rs):
        xb = x_ref[rs, :].astype(BF16)
        return (jnp.dot(xb, wg_ref[...], preferred_element_type=F32),
                jnp.dot(xb, wu_ref[...], preferred_element_type=F32))

    def down(gu):
        return jnp.dot((_silu(gu[0]) * gu[1]).astype(BF16), wd_ref[...], preferred_element_type=F32)

    gu = gate_up(rows[0])
    f_prev = None
    for i, rs in enumerate(rows):
        gu_next = gate_up(rows[i + 1]) if i + 1 < n_sub else None
        f = down(gu)
        if f_prev is not None:
            o_ref[rows[i - 1], :] = _layer_norm(ALPHA * x_ref[rows[i - 1], :] + 0.5 * f_prev, g_ref[...], b_ref[...])
        gu, f_prev = gu_next, f
    o_ref[rows[-1], :] = _layer_norm(ALPHA * x_ref[rows[-1], :] + 0.5 * f_prev, g_ref[...], b_ref[...])


def _ffn_ln(x, wg, wu, wd, layer, g, b):
    n, d = x.shape
    d_ff = wg.shape[2]
    tm = min(WIDE_TOKEN_TILE, n)
    stacked = lambda r, c: pl.BlockSpec((None, r, c), lambda i: (layer, 0, 0), pipeline_mode=pl.Buffered(1))
    return pl.pallas_call(
        _ffn_ln_body,
        grid=(n // tm,),
        in_specs=[pl.BlockSpec((tm, d), lambda i: (i, 0)),
                  stacked(d, d_ff), stacked(d, d_ff), stacked(d_ff, d),
                  _resident((1, d)), _resident((1, d))],
        out_specs=pl.BlockSpec((tm, d), lambda i: (i, 0)),
        out_shape=jax.ShapeDtypeStruct((n, d), F32),
        name="ffn_ln",
    )(x, wg, wu, wd, g, b)


def _rotary(x, cos, sin_signed):
    width = x.shape[-1]
    half = DK_R // 2
    lane = lax.broadcasted_iota(jnp.int32, x.shape, 1)
    other = jnp.where((lane % DK_R) < half,
                      pltpu.roll(x, width - half, 1),
                      pltpu.roll(x, half, 1))
    return x * cos + other * sin_signed


def _log_sigmoid(x):
    return jnp.minimum(x, 0.0) - jnp.log1p(jnp.exp(-jnp.abs(x)))


def _in_proj_body(h_ref, cos_ref, sin_ref, w_rq, w_rk, w_rv, w_rg, w_mq, w_gl, w_fq, w_fk, w_fv, w_fl, bf_ref,
                  bg_ref, *rest, prefill):
    rq_o, rk_o, rv_o, rg_o, mq_o, gl_o = rest[:6]
    fox_o = rest[6:]
    hb = h_ref[...].astype(BF16)

    def proj(w):
        return jnp.dot(hb, w, preferred_element_type=F32)

    def proj_t(wt):
        return lax.dot_general(wt, hb, _NT, preferred_element_type=F32)

    cos, sin = cos_ref[...], sin_ref[...]
    rq_o[...] = _rotary(proj(w_rq[...]), cos, sin)
    rk_o[...] = _rotary(proj(w_rk[...]), cos, sin) * (DK_R ** -0.5)
    rv_o[...] = proj(w_rv[...]).astype(rv_o.dtype)
    rg_o[...] = proj(w_rg[...])
    mq_o[...] = proj(w_mq[...]).astype(mq_o.dtype)
    gl_o[...] = jax.nn.sigmoid(proj(w_gl[...]) + bg_ref[...]).astype(gl_o.dtype)
    if prefill:
        fqt_o, fk_o, fkt_o, fvt_o, fvt16_o, lft_o = fox_o
        fqt_o[0] = (proj_t(w_fq[...]) * (D_F ** -0.5 * LOG2E)).astype(BF16)
        fk_o[...] = lax.dot_general(hb, w_fk[...], _NT, preferred_element_type=F32).astype(BF16)
        fkt_o[0] = proj_t(w_fk[...])
        fvt = proj_t(w_fv[...])
        fvt_o[0] = fvt
        fvt16_o[0] = fvt.astype(BF16)
        lft_o[0] = _log_sigmoid(proj_t(w_fl[...]) + bf_ref[...])
    else:
        fq_o, fk_o, fv_o, lft_o = fox_o
        fq_o[...] = (proj(w_fq[...]) * (D_F ** -0.5)).astype(fq_o.dtype)
        fk_o[...] = proj(w_fk[...])
        fv_o[...] = proj(w_fv[...])
        lft_o[0] = _log_sigmoid(proj_t(w_fl[...]) + bf_ref[...])


def _in_proj(h, cos, sin, w, b_forget, b_gate, groups, prefill, act_dtype):
    n, d = h.shape
    lg = n // groups
    tm = min(TOKEN_TILE, lg)
    nt = lg // tm
    row = lambda c: pl.BlockSpec((tm, c), lambda g, i: (g * nt + i, 0))
    tab = pl.BlockSpec((tm, RQ), lambda g, i: (i, 0))
    colt = lambda c: pl.BlockSpec((1, c, tm), lambda g, i: (g, 0, i))
    rows = lambda c, dt: jax.ShapeDtypeStruct((n, c), dt)
    cols = lambda c, dt: jax.ShapeDtypeStruct((groups, c, lg), dt)
    if prefill:
        fox_specs = [colt(FQ), row(FQ), colt(FQ), colt(FQ), colt(FQ), colt(H_F)]
        fox_shapes = [cols(FQ, BF16), rows(FQ, BF16), cols(FQ, F32), cols(FQ, F32), cols(FQ, BF16),
                      cols(H_F, F32)]
    else:
        fox_specs = [row(FQ), row(FQ), row(FQ), colt(H_F)]
        fox_shapes = [rows(FQ, act_dtype), rows(FQ, F32), rows(FQ, F32), cols(H_F, F32)]
    gates = w["gl"].shape[1]
    weights = [w[k] for k in ("rq", "rk", "rv", "rg", "mq", "gl", "fq", "fk", "fv", "fl")]
    weights += [b_forget, b_gate.reshape(1, gates)]
    return pl.pallas_call(
        functools.partial(_in_proj_body, prefill=prefill),
        grid=(groups, nt),
        in_specs=[row(d), tab, tab] + [_resident(a.shape) for a in weights],
        out_specs=[row(RQ), row(RQ), row(RV), row(RV), row(MQ), row(gates)] + fox_specs,
        out_shape=[rows(RQ, F32), rows(RQ, F32), rows(RV, act_dtype), rows(RV, F32), rows(MQ, act_dtype),
                   rows(gates, BF16)] + fox_shapes,
        name="in_proj",
    )(h, cos, sin, *weights)


def _retention_body(rq_ref, rk_ref, rv_ref, rg_ref, s0_ref, dmask_ref, qdec_ref, kdec_ref, gn_ref,
                    o_ref, sfin_ref, s_scr, *, chunk_decay):
    c = pl.program_id(1)

    @pl.when(c == 0)
    def _():
        s_scr[...] = s0_ref[...]

    chunk = dmask_ref.shape[1]
    seqs = s_scr.shape[0]
    seq_rows = rq_ref.shape[0] // seqs
    n_sub = seq_rows // chunk
    ks = lambda h: slice(h * DK_R, (h + 1) * DK_R)
    vs = lambda h: slice(h * DV_R, (h + 1) * DV_R)
    rs = lambda sq, i: slice(sq * seq_rows + i * chunk, sq * seq_rows + (i + 1) * chunk)
    units = [(sq, h, i) for sq in range(seqs) for h in range(H_R) for i in range(n_sub)]
    decayed = {u: lax.dot_general(rq_ref[rs(u[0], u[2]), ks(u[1])].astype(BF16),
                                  rk_ref[rs(u[0], u[2]), ks(u[1])].astype(BF16), _NT,
                                  preferred_element_type=F32) * dmask_ref[u[1]] for u in units}
    intra, kv = {}, {}
    for sq, h, i in units:
        vh = rv_ref[rs(sq, i), vs(h)].astype(BF16)
        intra[sq, h, i] = jnp.dot(decayed[sq, h, i].astype(BF16), vh, preferred_element_type=F32)
        kv[sq, h, i] = lax.dot_general((rk_ref[rs(sq, i), ks(h)] * kdec_ref[:, ks(h)]).astype(BF16), vh, _TN,
                                       preferred_element_type=F32)
    for sq, h in [(sq, h) for sq in range(seqs) for h in range(H_R)]:
        s_h = s_scr[sq, h]
        for i in range(n_sub):
            o = intra[sq, h, i] + jnp.dot((rq_ref[rs(sq, i), ks(h)] * qdec_ref[:, ks(h)]).astype(BF16),
                                          s_h.astype(BF16), preferred_element_type=F32)
            s_h = s_h * chunk_decay[h] + kv[sq, h, i]
            mu = jnp.mean(o, axis=-1, keepdims=True)
            oc = o - mu
            var = jnp.mean(oc * oc, axis=-1, keepdims=True)
            o_ref[rs(sq, i), vs(h)] = (_silu(rg_ref[rs(sq, i), vs(h)])
                                       * (oc * lax.rsqrt(var + GN_EPS) * gn_ref[:, vs(h)])).astype(o_ref.dtype)
        s_scr[sq, h] = s_h

    @pl.when(c == pl.num_programs(1) - 1)
    def _():
        sfin_ref[...] = s_scr[...]


def _retention_tables(chunk):
    log_g = jnp.log1p(-jnp.exp2(-5.0 - jnp.arange(H_R, dtype=F32)))
    i = jnp.arange(chunk, dtype=F32)
    diff = i[:, None] - i[None, :]
    dmask = jnp.where(diff[None] >= 0, jnp.exp(jnp.maximum(diff, 0.0)[None] * log_g[:, None, None]), 0.0)
    q_dec = jnp.exp((i + 1.0)[:, None] * log_g[None, :])
    k_dec = jnp.exp((chunk - 1.0 - i)[:, None] * log_g[None, :])
    widen = lambda t: jnp.repeat(t, DK_R, axis=1)
    return dmask, widen(q_dec), widen(k_dec)


def _chunk_decay(chunk):
    log_g = np.log1p(-np.exp2(-5.0 - np.arange(H_R, dtype=np.float32))).astype(np.float32)
    return tuple(float(v) for v in np.exp(np.float32(chunk) * log_g).astype(np.float32))


def _retention(rq, rk, rv, rg, s0, gn_g):
    batch = s0.shape[0]
    n = rq.shape[0]
    length = n // batch
    chunk = math.gcd(length, RET_CHUNK)
    rows = chunk * math.gcd(length // chunk, RET_CHUNKS_PER_STEP)
    nc = length // rows
    short = nc == 1 and length * SHORT_SEQS_PER_STEP <= RET_CHUNK * RET_CHUNKS_PER_STEP
    seqs = math.gcd(batch, SHORT_SEQS_PER_STEP) if short else 1
    dmask, q_dec, k_dec = _retention_tables(chunk)
    row = lambda c: pl.BlockSpec((seqs * rows, c), lambda b, i: (b * nc + i, 0))
    state = pl.BlockSpec((seqs, H_R, DK_R, DV_R), lambda b, i: (b, 0, 0, 0))
    return pl.pallas_call(
        functools.partial(_retention_body, chunk_decay=_chunk_decay(chunk)),
        grid=(batch // seqs, nc),
        in_specs=[row(RQ), row(RQ), row(RV), row(RV), state,
                  _resident((H_R, chunk, chunk)), _resident((chunk, RQ)), _resident((chunk, RQ)),
                  _resident((1, RV))],
        out_specs=[row(RV), state],
        out_shape=[jax.ShapeDtypeStruct((n, RV), rv.dtype),
                   jax.ShapeDtypeStruct((batch, H_R, DK_R, DV_R), F32)],
        scratch_shapes=[pltpu.VMEM((seqs, H_R, DK_R, DV_R), F32)],
        name="retention",
    )(rq, rk, rv, rg, s0, dmask, q_dec, k_dec, gn_g)


def _lane_suffix_sum(x):
    lane = lax.broadcasted_iota(jnp.int32, x.shape, 1)
    d = 1
    while d < LANES:
        x = x + jnp.where(lane < LANES - d, pltpu.roll(x, LANES - d, 1), 0.0)
        d *= 2
    return x


def _bf16_pieces(x):
    pieces = []
    for _ in range(BIAS_PIECES):
        piece = x.astype(BF16).astype(F32)
        pieces.append(piece)
        x = x - piece
    return jnp.concatenate(pieces, axis=0).astype(BF16)


def _forget_bias_body(lft_ref, tri_ref, place_ref, o_ref, carry_scr):
    @pl.when(pl.program_id(1) == 0)
    def _():
        carry_scr[...] = jnp.zeros(carry_scr.shape, F32)

    tokens = lft_ref.shape[2]
    c = carry_scr[...] + jnp.dot(_bf16_pieces(lft_ref[0]), tri_ref[...], preferred_element_type=F32
                                 ).reshape(BIAS_PIECES, H_F, tokens).sum(axis=0)
    carry_scr[...] = c[:, tokens - 1:tokens]
    planes = lax.dot_general(_bf16_pieces(c * (-LOG2E)), place_ref[...], _TN,
                             preferred_element_type=F32)
    for j in range(H_F // 2):
        o_ref[0, j] = planes[:, j * 2 * D_F:(j + 1) * 2 * D_F].astype(BF16)


def _bias_placement():
    plane_w = 2 * D_F
    place = np.zeros((BIAS_PIECES, H_F, (H_F // 2) * plane_w), np.float32)
    for i in range(BIAS_PIECES):
        for j in range(H_F // 2):
            place[i, 2 * j, j * plane_w + D_F + i] = 1.0
            place[i, 2 * j + 1, j * plane_w + i] = 1.0
    return jnp.asarray(place.reshape(BIAS_PIECES * H_F, -1), BF16)


def _forget_bias(lft):
    batch, _, length = lft.shape
    tokens = min(ATTN_TILE, length)
    nt = length // tokens
    tri = jnp.asarray(np.triu(np.ones((tokens, tokens), np.float32)), BF16)
    place = _bias_placement()
    return pl.pallas_call(
        _forget_bias_body, grid=(batch, nt),
        in_specs=[pl.BlockSpec((1, H_F, tokens), lambda b, i: (b, 0, i)), _resident(tri.shape),
                  _resident(place.shape)],
        out_specs=pl.BlockSpec((1, H_F // 2, tokens, 2 * D_F), lambda b, i: (b, 0, i, 0)),
        out_shape=jax.ShapeDtypeStruct((batch, H_F // 2, length, 2 * D_F), BF16),
        scratch_shapes=[pltpu.VMEM((H_F, 1), F32)],
        name="forget_bias",
    )(lft, tri, place)


def _fox_prefill_body(qi_tab, ki_tab, qt_ref, k_ref, kb_ref, vt_ref, o_ref, qa_scr, m_scr, acc_scr, *, heads):
    step_id = pl.program_id(2)
    qi = qi_tab[step_id]
    ki = ki_tab[step_id]
    pair_w = 2 * D_F
    acc_rows = D_F + BF16_ROWS

    @pl.when(ki == 0)
    def _():
        m_scr[...] = jnp.full(m_scr.shape, NEG_BIG, F32)
        acc_scr[...] = jnp.zeros(acc_scr.shape, F32)
        row = lax.broadcasted_iota(jnp.int32, (pair_w, qt_ref.shape[2]), 0)
        for h in range(heads):
            pair = qt_ref[0, (h // 2) * pair_w:(h // 2 + 1) * pair_w, :]
            first = D_F if h % 2 == 0 else 0
            qa_scr[h] = jnp.where((row >= first) & (row < first + BIAS_PIECES), jnp.ones_like(pair), pair)

    def step(masked):
        lane = lax.broadcasted_iota(jnp.int32, (k_ref.shape[0], pair_w), 1)
        ones = jnp.ones((BF16_ROWS, k_ref.shape[0]), BF16)

        def logits(h):
            j = h // 2
            own = (lane < D_F) if h % 2 == 0 else (lane >= D_F)
            ka = jnp.where(own, k_ref[:, j * pair_w:(j + 1) * pair_w], kb_ref[0, j])
            return jnp.dot(ka, qa_scr[h], preferred_element_type=F32)

        ahead = [logits(h) for h in range(min(PREFILL_LOOKAHEAD, heads))]
        for h in range(heads):
            st = ahead.pop(0)
            if h + PREFILL_LOOKAHEAD < heads:
                ahead.append(logits(h + PREFILL_LOOKAHEAD))
            if masked:
                key = lax.broadcasted_iota(jnp.int32, st.shape, 0)
                qry = lax.broadcasted_iota(jnp.int32, st.shape, 1)
                st = jnp.where(key <= qry, st, NEG_BIG)
            m_old = m_scr[h]
            m_new = jnp.maximum(m_old, jnp.max(st, axis=0, keepdims=True))
            alpha = jnp.exp2(m_old - m_new)
            pt = jnp.exp2(st - m_new)
            va = jnp.concatenate([vt_ref[0, h * D_F:(h + 1) * D_F, :], ones], axis=0)
            rows = slice(h * acc_rows, (h + 1) * acc_rows)
            acc_scr[rows, :] = alpha * acc_scr[rows, :] + jnp.dot(va, pt.astype(BF16),
                                                                  preferred_element_type=F32)
            m_scr[h] = m_new

    @pl.when(ki < qi)
    def _():
        step(False)

    @pl.when(ki == qi)
    def _():
        step(True)
        out_t = jnp.concatenate([acc_scr[h * acc_rows:h * acc_rows + D_F, :]
                                 / acc_scr[h * acc_rows + D_F:h * acc_rows + D_F + 1, :] for h in range(heads)],
                                axis=0)
        o_ref[...] = out_t.T.astype(BF16)


def _fox_prefill(fqt, fk, kbias, fvt):
    batch, _, length = fqt.shape
    tile = min(ATTN_TILE, length)
    nt = length // tile
    heads = PREFILL_HEADS
    width = heads * D_F
    pairs = [(q, k) for q in range(nt) for k in range(q + 1)]
    qi_tab = jnp.asarray([q for q, _ in pairs], jnp.int32)
    ki_tab = jnp.asarray([k for _, k in pairs], jnp.int32)
    grid_spec = pltpu.PrefetchScalarGridSpec(
        num_scalar_prefetch=2,
        grid=(batch, H_F // heads, len(pairs)),
        in_specs=[pl.BlockSpec((1, width, tile), lambda b, g, p, qt, kt: (b, g, qt[p])),
                  pl.BlockSpec((tile, width), lambda b, g, p, qt, kt: (b * nt + kt[p], g)),
                  pl.BlockSpec((1, heads // 2, tile, 2 * D_F), lambda b, g, p, qt, kt: (b, g, kt[p], 0)),
                  pl.BlockSpec((1, width, tile), lambda b, g, p, qt, kt: (b, g, kt[p]))],
        out_specs=pl.BlockSpec((tile, width), lambda b, g, p, qt, kt: (b * nt + qt[p], g)),
        scratch_shapes=[pltpu.VMEM((heads, 2 * D_F, tile), BF16), pltpu.VMEM((heads, 1, tile), F32),
                        pltpu.VMEM((heads * (D_F + BF16_ROWS), tile), F32)],
    )
    return pl.pallas_call(
        functools.partial(_fox_prefill_body, heads=heads),
        grid_spec=grid_spec,
        out_shape=jax.ShapeDtypeStruct(fk.shape, BF16),
        name="fox_prefill",
    )(qi_tab, ki_tab, fqt, fk, kbias, fvt)


def _fox_decode_body(pt_ref, q_ref, kn_ref, vn_ref, lfn_ref, k_hbm, v_hbm, lf_hbm, o_ref,
                     k_buf, v_buf, lf_buf, sems, qbd_scr, m_scr, l_scr, acc_scr, r_scr, *, pages, t_new, steps,
                     total):
    b_idx = pl.program_id(0)
    p_idx = pl.program_id(1)
    n_pages = steps * pages
    rows = H_F * t_new
    half = pages // DECODE_GROUPS
    step_id = b_idx * steps + p_idx
    slot = step_id % DECODE_SLOTS

    def page_copies(step):
        b, p, slot = step // steps, step % steps, step % DECODE_SLOTS
        out = []
        for j in range(pages):
            page = pt_ref[b, n_pages - 1 - (p * pages + j)]
            out.append(pltpu.make_async_copy(k_hbm.at[page], k_buf.at[slot, j], sems.at[0, slot]))
            out.append(pltpu.make_async_copy(v_hbm.at[page], v_buf.at[slot, j], sems.at[1, slot]))
            out.append(pltpu.make_async_copy(lf_hbm.at[page], lf_buf.at[slot, j], sems.at[2, slot]))
        return out

    @pl.when(step_id == 0)
    def _():
        for ahead in range(min(DECODE_SLOTS - 1, total)):
            for c in page_copies(step_id + ahead):
                c.start()

    @pl.when(step_id + DECODE_SLOTS - 1 < total)
    def _():
        for c in page_copies(step_id + DECODE_SLOTS - 1):
            c.start()

    def expand(x):
        return jnp.broadcast_to(x[:, None, :], (H_F, t_new, x.shape[-1])).reshape(rows, x.shape[-1])

    @pl.when(p_idx == 0)
    def _():
        q = jnp.concatenate([q_ref[...].astype(F32)] * H_F, axis=0)
        r_head = lax.broadcasted_iota(jnp.int32, (rows, FQ), 0) // t_new
        l_head = lax.broadcasted_iota(jnp.int32, (rows, FQ), 1) // D_F
        qbd = jnp.where(r_head == l_head, q, 0.0).astype(BF16)
        qbd_scr[...] = qbd
        lfn = lfn_ref[0]
        lane = lax.broadcasted_iota(jnp.int32, lfn.shape, 1)
        cnew = jnp.zeros_like(lfn)
        for i in range(t_new):
            cnew = cnew + jnp.where(lane >= i, lfn[:, i:i + 1], 0.0)
        s = lax.dot_general(qbd, kn_ref[...].astype(BF16), _NT, preferred_element_type=F32) - expand(cnew)
        t_q = lax.broadcasted_iota(jnp.int32, s.shape, 0) % t_new
        t_k = lax.broadcasted_iota(jnp.int32, s.shape, 1)
        s = jnp.where(t_k <= t_q, s, NEG_BIG)
        m = jnp.max(s, axis=-1, keepdims=True)
        p = jnp.exp(s - m)
        m_scr[...] = jnp.full(m_scr.shape, NEG_BIG, F32)
        l_scr[...] = jnp.zeros(l_scr.shape, F32)
        acc_scr[...] = jnp.zeros(acc_scr.shape, F32)
        m_scr[0] = m
        l_scr[0] = jnp.sum(p, axis=-1, keepdims=True)
        acc_scr[0] = jnp.dot(p.astype(BF16), vn_ref[...].astype(BF16), preferred_element_type=F32)
        r_scr[...] = jnp.zeros(r_scr.shape, F32)

    for c in page_copies(step_id):
        c.wait()

    qbd = qbd_scr[...]
    run = r_scr[...]
    biases = []
    for j in range(pages):
        lf = lf_buf[slot, j]
        suffix = _lane_suffix_sum(lf)
        biases.append(expand(suffix - lf + run))
        run = run + suffix[:, 0:1]
    r_scr[...] = run
    for g in range(DECODE_GROUPS):
        js = range(g * half, (g + 1) * half)
        s = jnp.concatenate(
            [jnp.dot(qbd, k_buf[slot, j].reshape(FQ, LANES).astype(BF16), preferred_element_type=F32)
             + biases[j] for j in js], axis=-1)
        m_old = m_scr[g]
        m_new = jnp.maximum(m_old, jnp.max(s, axis=-1, keepdims=True))
        alpha = jnp.exp(m_old - m_new)
        p = jnp.exp(s - m_new)
        l_scr[g] = alpha * l_scr[g] + jnp.sum(p, axis=-1, keepdims=True)
        p = p.astype(BF16)
        acc = alpha * acc_scr[g]
        for i, j in enumerate(js):
            acc = acc + lax.dot_general(p[:, i * LANES:(i + 1) * LANES],
                                        v_buf[slot, j].reshape(FQ, LANES).astype(BF16), _NT,
                                        preferred_element_type=F32)
        acc_scr[g] = acc
        m_scr[g] = m_new

    @pl.when(p_idx == steps - 1)
    def _():
        m = jnp.max(m_scr[...], axis=0)
        scale = jnp.exp(m_scr[...] - m)
        full = jnp.sum(scale * acc_scr[...], axis=0) / jnp.sum(scale * l_scr[...], axis=0)
        l_head = lax.broadcasted_iota(jnp.int32, (t_new, FQ), 1) // D_F
        out = jnp.zeros((t_new, FQ), F32)
        for h in range(H_F):
            out = out + jnp.where(l_head == h, full[h * t_new:(h + 1) * t_new, :], 0.0)
        o_ref[...] = out.astype(o_ref.dtype)


def _fox_decode(fq, fk, fv, lf_new, cache_kt, cache_vt, cache_lft, page_table):
    db, n_pages = page_table.shape
    t_new = fq.shape[0] // db
    pages = math.gcd(PAGES_PER_STEP, n_pages)
    assert pages % DECODE_GROUPS == 0
    steps = n_pages // pages
    rows = H_F * t_new
    page_len = cache_kt.shape[-1]
    assert page_len == LANES
    tok = pl.BlockSpec((t_new, FQ), lambda b, p, pt: (b, 0))
    hbm = pl.BlockSpec(memory_space=pltpu.HBM)
    grid_spec = pltpu.PrefetchScalarGridSpec(
        num_scalar_prefetch=1,
        grid=(db, steps),
        in_specs=[tok, tok, tok, pl.BlockSpec((1, H_F, t_new), lambda b, p, pt: (b, 0, 0)), hbm, hbm, hbm],
        out_specs=tok,
        scratch_shapes=[pltpu.VMEM((DECODE_SLOTS, pages, H_F, D_F, page_len), F32),
                        pltpu.VMEM((DECODE_SLOTS, pages, H_F, D_F, page_len), F32),
                        pltpu.VMEM((DECODE_SLOTS, pages, H_F, page_len), F32),
                        pltpu.SemaphoreType.DMA((3, DECODE_SLOTS)),
                        pltpu.VMEM((rows, FQ), BF16),
                        pltpu.VMEM((DECODE_GROUPS, rows, 1), F32), pltpu.VMEM((DECODE_GROUPS, rows, 1), F32),
                        pltpu.VMEM((DECODE_GROUPS, rows, FQ), F32), pltpu.VMEM((H_F, 1), F32)],
    )
    return pl.pallas_call(
        functools.partial(_fox_decode_body, pages=pages, t_new=t_new, steps=steps, total=db * steps),
        grid_spec=grid_spec,
        out_shape=jax.ShapeDtypeStruct(fq.shape, fq.dtype),
        compiler_params=pltpu.CompilerParams(dimension_semantics=("arbitrary", "arbitrary")),
        name="fox_decode",
    )(page_table, fq, fk, fv, lf_new, cache_kt, cache_vt, cache_lft)


def _mem_attn_body(q_ref, k_ref, v_ref, o_ref):
    seqs = k_ref.shape[0]
    tq = q_ref.shape[0] // seqs
    n_mem = k_ref.shape[1] // H_M
    units = [(sq, h) for sq in range(seqs) for h in range(H_M)]
    rows = lambda sq: slice(sq * tq, (sq + 1) * tq)
    lanes = lambda h: slice(h * D_M, (h + 1) * D_M)
    head_rows = lambda h: pl.ds(h, n_mem, stride=H_M)
    scores = [lax.dot_general(q_ref[rows(sq), lanes(h)].astype(BF16), k_ref[sq, head_rows(h), :].astype(BF16),
                              _NT, preferred_element_type=F32) * (D_M ** -0.5) for sq, h in units]
    for (sq, h), s in zip(units, scores):
        p = jnp.exp(s - jnp.max(s, axis=-1, keepdims=True))
        l = jnp.sum(p, axis=-1, keepdims=True)
        o_ref[rows(sq), lanes(h)] = (jnp.dot(p.astype(BF16), v_ref[sq, head_rows(h), :].astype(BF16),
                                             preferred_element_type=F32) / l).astype(o_ref.dtype)


def _mem_attn(mq, mem_k, mem_v):
    batch, mem_rows, _ = mem_k.shape
    length = mq.shape[0] // batch
    tq = min(ATTN_TILE, length)
    nt = length // tq
    seqs = math.gcd(batch, SHORT_SEQS_PER_STEP) if nt == 1 and length * SHORT_SEQS_PER_STEP <= ATTN_TILE else 1
    row = pl.BlockSpec((seqs * tq, MQ), lambda b, i: (b * nt + i, 0))
    mem = pl.BlockSpec((seqs, mem_rows, D_M), lambda b, i: (b, 0, 0))
    return pl.pallas_call(_mem_attn_body, grid=(batch // seqs, nt), in_specs=[row, mem, mem], out_specs=row,
                          out_shape=jax.ShapeDtypeStruct(mq.shape, mq.dtype), name="mem_attn")(mq, mem_k, mem_v)


def _mem_kv_body(x_ref, w_ref, k_ref, v_ref):
    tm = x_ref.shape[0]
    kv = jnp.dot(x_ref[...].astype(BF16), w_ref[...], preferred_element_type=F32)
    for h in range(H_M):
        head_rows = pl.ds(h, tm, stride=H_M)
        k_ref[head_rows, :] = kv[:, h * D_M:(h + 1) * D_M]
        v_ref[head_rows, :] = kv[:, MQ + h * D_M:MQ + (h + 1) * D_M]


def _mem_kv(x, w):
    n, d = x.shape
    tm = min(TOKEN_TILE, n)
    out = pl.BlockSpec((tm * H_M, D_M), lambda i: (i, 0))
    shape = jax.ShapeDtypeStruct((n * H_M, D_M), F32)
    return pl.pallas_call(
        _mem_kv_body, grid=(n // tm,),
        in_specs=[pl.BlockSpec((tm, d), lambda i: (i, 0)), _resident(w.shape)],
        out_specs=[out, out], out_shape=[shape, shape], name="mem_kv_proj")(x, w)


def _merge_body(h_ref, ret_ref, fox_ref, mem_ref, gate_ref, w_ret, w_fox, w_mem, w_out, g_ref, b_ref, o_ref):
    d = h_ref.shape[1]
    n_sub = max(1, h_ref.shape[0] // FFN_SUB_ROWS)
    sub = h_ref.shape[0] // n_sub
    rows = [slice(i * sub, (i + 1) * sub) for i in range(n_sub)]

    def branches(rs):
        return [jnp.dot(x_ref[rs, :].astype(BF16), w_ref[...], preferred_element_type=F32)
                for x_ref, w_ref in ((ret_ref, w_ret), (fox_ref, w_fox), (mem_ref, w_mem))]

    def mixed(rs, br):
        mix = sum(gate_ref[rs, i * d:(i + 1) * d].astype(F32) * b for i, b in enumerate(br))
        return jnp.dot(mix.astype(BF16), w_out[...], preferred_element_type=F32)

    br = branches(rows[0])
    for i, rs in enumerate(rows):
        br_next = branches(rows[i + 1]) if i + 1 < n_sub else None
        o_ref[rs, :] = _layer_norm(ALPHA * h_ref[rs, :] + mixed(rs, br), g_ref[...], b_ref[...])
        br = br_next


def _merge(h, ret, fox, mem, gate, w_ret, w_fox, w_mem, w_out, g, b):
    n, d = h.shape
    tm = min(WIDE_TOKEN_TILE, n)
    row = lambda c: pl.BlockSpec((tm, c), lambda i: (i, 0))
    return pl.pallas_call(
        _merge_body, grid=(n // tm,),
        in_specs=[row(d), row(RV), row(FQ), row(MQ), row(N_BRANCH * d),
                  _resident(w_ret.shape), _resident(w_fox.shape), _resident(w_mem.shape), _resident(w_out.shape),
                  _resident((1, d)), _resident((1, d))],
        out_specs=row(d),
        out_shape=jax.ShapeDtypeStruct((n, d), F32), name="merge",
    )(h, ret, fox, mem, gate, w_ret, w_fox, w_mem, w_out, g, b)


def _rope_tables(pos):
    half = DK_R // 2
    inv = ROPE_BASE ** (-jnp.arange(half, dtype=F32) / half)
    ang = pos[:, None] * inv[None, :]
    cos, sin = jnp.cos(ang), jnp.sin(ang)
    return (jnp.tile(jnp.concatenate([cos, cos], axis=1), (1, H_R)),
            jnp.tile(jnp.concatenate([-sin, sin], axis=1), (1, H_R)))


def _split_w_in(w_in, d_model, prefill):
    sizes = (RQ, RQ, RV, RV, FQ, FQ, FQ, H_F, MQ, N_BRANCH * d_model)
    names = ("rq", "rk", "rv", "rg", "fq", "fk", "fv", "fl", "mq", "gl")
    offs = np.concatenate([[0], np.cumsum(sizes)])
    w = {nm: w_in[:, int(offs[i]):int(offs[i + 1])].astype(BF16) for i, nm in enumerate(names)}
    if prefill:
        w["fq"], w["fk"], w["fv"] = w["fq"].T, w["fk"].T, w["fv"].T
    w["fl"] = w["fl"].T
    return w


def _layer(x, pos, groups, s0, mem_k, mem_v, fox_attend, prefill, p):
    d = x.shape[1]
    ln_g, ln_b = p["ln_g"], p["ln_b"]
    act_dtype = BF16 if (x.shape[0] // s0.shape[0]) % BF16_ROWS == 0 else F32
    h = _ffn_ln(x, p["wg"], p["wu"], p["wd"], 0, ln_g[0:1], ln_b[0:1])
    cos, sin = _rope_tables(pos)
    rq, rk, rv, rg, mq, gl, *fox_in = _in_proj(
        h, cos, sin, _split_w_in(p["w_in"], d, prefill), p["b_forget"], p["b_gate"], groups, prefill, act_dtype)
    ret, ret_state = _retention(rq, rk, rv, rg, s0, p["ret_gn_g"])
    fox = fox_attend(*fox_in)
    mem = _mem_attn(mq, mem_k, mem_v)
    h = _merge(h, ret, fox, mem, gl, p["w_ret_o"], p["w_fox_o"], p["w_mem_o"], p["w_out"], ln_g[1:2], ln_b[1:2])
    y = _ffn_ln(h, p["wg"], p["wu"], p["wd"], 1, ln_g[2:3], ln_b[2:3])
    return y, fox_in, ret_state


def kernel(x_prompt, x_sample, mem_prompt, cache_fox_k, cache_fox_v, cache_fox_logf, state_ret, cache_mem_k, cache_mem_v, page_table, ln_g, ln_b, w_ffn_gate, w_ffn_up, w_ffn_down, w_in, b_forget, b_gate, ret_gn_g, w_ret_o, w_fox_o, w_mem_k, w_mem_v, w_mem_o, w_out):
    bp, lp, d = x_prompt.shape
    db, ls, _ = x_sample.shape
    n_mem = mem_prompt.shape[1]
    n_pages = page_table.shape[1]
    page = cache_fox_k.shape[1]
    past_len = n_pages * page
    params = {
        "ln_g": ln_g, "ln_b": ln_b,
        "wg": w_ffn_gate.astype(BF16), "wu": w_ffn_up.astype(BF16), "wd": w_ffn_down.astype(BF16),
        "w_in": w_in, "b_forget": b_forget.reshape(H_F, 1), "b_gate": b_gate,
        "ret_gn_g": ret_gn_g.reshape(1, RV),
        "w_ret_o": w_ret_o.astype(BF16), "w_fox_o": w_fox_o.astype(BF16), "w_mem_o": w_mem_o.astype(BF16),
        "w_out": w_out.astype(BF16),
    }

    mem_k_p, mem_v_p = (t.reshape(bp, n_mem * H_M, D_M) for t in _mem_kv(
        mem_prompt.reshape(bp * n_mem, d), jnp.concatenate([w_mem_k, w_mem_v], axis=1).astype(BF16)))

    def attend_prompt(fqt, fk, fkt, fvt, fvt16, lft):
        return _fox_prefill(fqt, fk, _forget_bias(lft), fvt16)

    y_p, (_, _, fkt_p, fvt_p, _, lft_p), ret_p = _layer(
        x_prompt.reshape(bp * lp, d), jnp.arange(lp, dtype=F32), bp, jnp.zeros((bp, H_R, DK_R, DV_R), F32),
        mem_k_p, mem_v_p, attend_prompt, True, params)

    cache_kt = jnp.transpose(cache_fox_k, (0, 2, 3, 1))
    cache_vt = jnp.transpose(cache_fox_v, (0, 2, 3, 1))
    cache_lft = jnp.transpose(cache_fox_logf, (0, 2, 1))

    def attend_sample(fq, fk, fv, lft):
        lf_new = jnp.transpose(lft.reshape(H_F, db, ls), (1, 0, 2))
        return _fox_decode(fq, fk, fv, lf_new, cache_kt, cache_vt, cache_lft, page_table)

    y_s, (_, fk_s, fv_s, lft_s), ret_s = _layer(
        x_sample.reshape(db * ls, d), past_len + jnp.tile(jnp.arange(ls, dtype=F32), db), 1, state_ret,
        cache_mem_k.reshape(db, n_mem * H_M, D_M), cache_mem_v.reshape(db, n_mem * H_M, D_M), attend_sample,
        False, params)

    to_heads = lambda t, b, l: jnp.transpose(t.reshape(b, H_F, D_F, l), (0, 3, 1, 2))
    return (y_p.reshape(bp, lp, d), y_s.reshape(db, ls, d),
            to_heads(fkt_p, bp, lp), to_heads(fvt_p, bp, lp), jnp.transpose(lft_p, (0, 2, 1)),
            ret_p, mem_k_p.reshape(bp, n_mem, H_M, D_M), mem_v_p.reshape(bp, n_mem, H_M, D_M),
            fk_s.reshape(db, ls, H_F, D_F), fv_s.reshape(db, ls, H_F, D_F),
            jnp.transpose(lft_s.reshape(H_F, db, ls), (1, 2, 0)), ret_s)
```

```python
import functools
import math

import jax
import jax.numpy as jnp
import numpy as np
from jax import lax
from jax.experimental import pallas as pl
from jax.experimental.pallas import tpu as pltpu

F32 = jnp.float32
BF16 = jnp.bfloat16

DEPTH = 1
H_R, DK_R, DV_R = 4, 64, 128
RET_CHUNK = 128
RET_CHUNKS_PER_STEP = 4
SHORT_SEQS_PER_STEP = 8
ROPE_BASE = 10000.0
H_F, D_F = 8, 64
H_M, D_M = 4, 128
N_BRANCH = 3
LN_EPS = 1e-5
GN_EPS = 1e-5
ALPHA = (2.0 * DEPTH) ** 0.25
RQ, RV, FQ, MQ = H_R * DK_R, H_R * DV_R, H_F * D_F, H_M * D_M

LANES = 128
BF16_ROWS = 16
TOKEN_TILE = 256
WIDE_TOKEN_TILE = 512
FFN_SUB_ROWS = 256
ATTN_TILE = 512
PREFILL_LOOKAHEAD = 2
PREFILL_HEADS = 8
PAGES_PER_STEP = 16
DECODE_GROUPS = 2
DECODE_SLOTS = 3
NEG_BIG = -1e30
LOG2E = math.log2(math.e)
BIAS_TOKENS = 1024
BIAS_PIECES = 3

_NT = (((1,), (1,)), ((), ()))
_TN = (((0,), (0,)), ((), ()))


def _resident(shape):
    return pl.BlockSpec(shape, lambda *_: (0,) * len(shape), pipeline_mode=pl.Buffered(1))


def _layer_norm(y, g, b):
    mu = jnp.mean(y, axis=-1, keepdims=True)
    yc = y - mu
    var = jnp.mean(yc * yc, axis=-1, keepdims=True)
    return yc * lax.rsqrt(var + LN_EPS) * g + b


def _silu(x):
    return x * jax.nn.sigmoid(x)


def _ffn_ln_body(x_ref, wg_ref, wu_ref, wd_ref, g_ref, b_ref, o_ref):
    n_sub = max(1, x_ref.shape[0] // FFN_SUB_ROWS)
    rows = [slice(i * (x_ref.shape[0] // n_sub), (i + 1) * (x_ref.shape[0] // n_sub)) for i in range(n_sub)]

    def gate_up(rs):
        xb = x_ref[rs, :].astype(BF16)
        return (jnp.dot(xb, wg_ref[...], preferred_element_type=F32),
                jnp.dot(xb, wu_ref[...], preferred_element_type=F32))

    def down(gu):
        return jnp.dot((_silu(gu[0]) * gu[1]).astype(BF16), wd_ref[...], preferred_element_type=F32)

    gu = gate_up(rows[0])
    f_prev = None
    for i, rs in enumerate(rows):
        gu_next = gate_up(rows[i + 1]) if i + 1 < n_sub else None
        f = down(gu)
        if f_prev is not None:
            o_ref[rows[i - 1], :] = _layer_norm(ALPHA * x_ref[rows[i - 1], :] + 0.5 * f_prev, g_ref[...], b_ref[...])
        gu, f_prev = gu_next, f
    o_ref[rows[-1], :] = _layer_norm(ALPHA * x_ref[rows[-1], :] + 0.5 * f_prev, g_ref[...], b_ref[...])


def _ffn_ln(x, wg, wu, wd, layer, g, b):
    n, d = x.shape
    d_ff = wg.shape[2]
    tm = min(WIDE_TOKEN_TILE, n)
    stacked = lambda r, c: pl.BlockSpec((None, r, c), lambda i: (layer, 0, 0), pipeline_mode=pl.Buffered(1))
    return pl.pallas_call(
        _ffn_ln_body,
        grid=(n // tm,),
        in_specs=[pl.BlockSpec((tm, d), lambda i: (i, 0)),
                  stacked(d, d_ff), stacked(d, d_ff), stacked(d_ff, d),
                  _resident((1, d)), _resident((1, d))],
        out_specs=pl.BlockSpec((tm, d), lambda i: (i, 0)),
        out_shape=jax.ShapeDtypeStruct((n, d), F32),
        name="ffn_ln",
    )(x, wg, wu, wd, g, b)


def _rotary(x, cos, sin_signed):
    width = x.shape[-1]
    half = DK_R // 2
    lane = lax.broadcasted_iota(jnp.int32, x.shape, 1)
    other = jnp.where((lane % DK_R) < half,
                      pltpu.roll(x, width - half, 1),
                      pltpu.roll(x, half, 1))
    return x * cos + other * sin_signed


def _log_sigmoid(x):
    return jnp.minimum(x, 0.0) - jnp.log1p(jnp.exp(-jnp.abs(x)))


def _in_proj_body(h_ref, cos_ref, sin_ref, w_rq, w_rk, w_rv, w_rg, w_mq, w_gl, w_fq, w_fk, w_fv, w_fl, bf_ref,
                  bg_ref, *rest, prefill):
    rq_o, rk_o, rv_o, rg_o, mq_o, gl_o = rest[:6]
    fox_o = rest[6:]
    hb = h_ref[...].astype(BF16)

    def proj(w):
        return jnp.dot(hb, w, preferred_element_type=F32)

    def proj_t(wt):
        return lax.dot_general(wt, hb, _NT, preferred_element_type=F32)

    cos, sin = cos_ref[...], sin_ref[...]
    rq_o[...] = _rotary(proj(w_rq[...]), cos, sin)
    rk_o[...] = _rotary(proj(w_rk[...]), cos, sin) * (DK_R ** -0.5)
    rv_o[...] = proj(w_rv[...]).astype(rv_o.dtype)
    rg_o[...] = proj(w_rg[...])
    mq_o[...] = proj(w_mq[...]).astype(mq_o.dtype)
    gl_o[...] = jax.nn.sigmoid(proj(w_gl[...]) + bg_ref[...]).astype(gl_o.dtype)
    if prefill:
        fqt_o, fk_o, fkt_o, fvt_o, fvt16_o, lft_o = fox_o
        fqt_o[0] = (proj_t(w_fq[...]) * (D_F ** -0.5 * LOG2E)).astype(BF16)
        fk = proj(w_fk[...])
        fk_o[...] = fk.astype(BF16)
        fkt_o[0] = fk.T
        fvt = proj_t(w_fv[...])
        fvt_o[0] = fvt
        fvt16_o[0] = fvt.astype(BF16)
        lft_o[0] = _log_sigmoid(proj_t(w_fl[...]) + bf_ref[...])
    else:
        fq_o, fk_o, fv_o, lft_o = fox_o
        fq_o[...] = (proj(w_fq[...]) * (D_F ** -0.5)).astype(fq_o.dtype)
        fk_o[...] = proj(w_fk[...])
        fv_o[...] = proj(w_fv[...])
        lft_o[0] = _log_sigmoid(proj_t(w_fl[...]) + bf_ref[...])


def _in_proj(h, cos, sin, w, b_forget, b_gate, groups, prefill, act_dtype):
    n, d = h.shape
    lg = n // groups
    tm = min(TOKEN_TILE, lg)
    nt = lg // tm
    row = lambda c: pl.BlockSpec((tm, c), lambda g, i: (g * nt + i, 0))
    tab = pl.BlockSpec((tm, RQ), lambda g, i: (i, 0))
    colt = lambda c: pl.BlockSpec((1, c, tm), lambda g, i: (g, 0, i))
    rows = lambda c, dt: jax.ShapeDtypeStruct((n, c), dt)
    cols = lambda c, dt: jax.ShapeDtypeStruct((groups, c, lg), dt)
    if prefill:
        fox_specs = [colt(FQ), row(FQ), colt(FQ), colt(FQ), colt(FQ), colt(H_F)]
        fox_shapes = [cols(FQ, BF16), rows(FQ, BF16), cols(FQ, F32), cols(FQ, F32), cols(FQ, BF16),
                      cols(H_F, F32)]
    else:
        fox_specs = [row(FQ), row(FQ), row(FQ), colt(H_F)]
        fox_shapes = [rows(FQ, act_dtype), rows(FQ, F32), rows(FQ, F32), cols(H_F, F32)]
    gates = w["gl"].shape[1]
    weights = [w[k] for k in ("rq", "rk", "rv", "rg", "mq", "gl", "fq", "fk", "fv", "fl")]
    weights += [b_forget, b_gate.reshape(1, gates)]
    return pl.pallas_call(
        functools.partial(_in_proj_body, prefill=prefill),
        grid=(groups, nt),
        in_specs=[row(d), tab, tab] + [_resident(a.shape) for a in weights],
        out_specs=[row(RQ), row(RQ), row(RV), row(RV), row(MQ), row(gates)] + fox_specs,
        out_shape=[rows(RQ, F32), rows(RQ, F32), rows(RV, act_dtype), rows(RV, F32), rows(MQ, act_dtype),
                   rows(gates, BF16)] + fox_shapes,
        name="in_proj",
    )(h, cos, sin, *weights)


def _retention_body(rq_ref, rk_ref, rv_ref, rg_ref, s0_ref, dmask_ref, qdec_ref, kdec_ref, gn_ref,
                    o_ref, sfin_ref, s_scr, *, chunk_decay):
    c = pl.program_id(1)

    @pl.when(c == 0)
    def _():
        s_scr[...] = s0_ref[...]

    chunk = dmask_ref.shape[1]
    seqs = s_scr.shape[0]
    seq_rows = rq_ref.shape[0] // seqs
    n_sub = seq_rows // chunk
    ks = lambda h: slice(h * DK_R, (h + 1) * DK_R)
    vs = lambda h: slice(h * DV_R, (h + 1) * DV_R)
    rs = lambda sq, i: slice(sq * seq_rows + i * chunk, sq * seq_rows + (i + 1) * chunk)
    units = [(sq, h, i) for sq in range(seqs) for h in range(H_R) for i in range(n_sub)]
    decayed = {u: lax.dot_general(rq_ref[rs(u[0], u[2]), ks(u[1])].astype(BF16),
                                  rk_ref[rs(u[0], u[2]), ks(u[1])].astype(BF16), _NT,
                                  preferred_element_type=F32) * dmask_ref[u[1]] for u in units}
    intra, kv = {}, {}
    for sq, h, i in units:
        vh = rv_ref[rs(sq, i), vs(h)].astype(BF16)
        intra[sq, h, i] = jnp.dot(decayed[sq, h, i].astype(BF16), vh, preferred_element_type=F32)
        kv[sq, h, i] = lax.dot_general((rk_ref[rs(sq, i), ks(h)] * kdec_ref[:, ks(h)]).astype(BF16), vh, _TN,
                                       preferred_element_type=F32)
    for sq, h in [(sq, h) for sq in range(seqs) for h in range(H_R)]:
        s_h = s_scr[sq, h]
        for i in range(n_sub):
            o = intra[sq, h, i] + jnp.dot((rq_ref[rs(sq, i), ks(h)] * qdec_ref[:, ks(h)]).astype(BF16),
                                          s_h.astype(BF16), preferred_element_type=F32)
            s_h = s_h * chunk_decay[h] + kv[sq, h, i]
            mu = jnp.mean(o, axis=-1, keepdims=True)
            oc = o - mu
            var = jnp.mean(oc * oc, axis=-1, keepdims=True)
            o_ref[rs(sq, i), vs(h)] = (_silu(rg_ref[rs(sq, i), vs(h)])
                                       * (oc * lax.rsqrt(var + GN_EPS) * gn_ref[:, vs(h)])).astype(o_ref.dtype)
        s_scr[sq, h] = s_h

    @pl.when(c == pl.num_programs(1) - 1)
    def _():
        sfin_ref[...] = s_scr[...]


def _retention_tables(chunk):
    log_g = jnp.log1p(-jnp.exp2(-5.0 - jnp.arange(H_R, dtype=F32)))
    i = jnp.arange(chunk, dtype=F32)
    diff = i[:, None] - i[None, :]
    dmask = jnp.where(diff[None] >= 0, jnp.exp(jnp.maximum(diff, 0.0)[None] * log_g[:, None, None]), 0.0)
    q_dec = jnp.exp((i + 1.0)[:, None] * log_g[None, :])
    k_dec = jnp.exp((chunk - 1.0 - i)[:, None] * log_g[None, :])
    widen = lambda t: jnp.repeat(t, DK_R, axis=1)
    return dmask, widen(q_dec), widen(k_dec)


def _chunk_decay(chunk):
    log_g = np.log1p(-np.exp2(-5.0 - np.arange(H_R, dtype=np.float32))).astype(np.float32)
    return tuple(float(v) for v in np.exp(np.float32(chunk) * log_g).astype(np.float32))


def _retention(rq, rk, rv, rg, s0, gn_g):
    batch = s0.shape[0]
    n = rq.shape[0]
    length = n // batch
    chunk = math.gcd(length, RET_CHUNK)
    rows = chunk * math.gcd(length // chunk, RET_CHUNKS_PER_STEP)
    nc = length // rows
    short = nc == 1 and length * SHORT_SEQS_PER_STEP <= RET_CHUNK * RET_CHUNKS_PER_STEP
    seqs = math.gcd(batch, SHORT_SEQS_PER_STEP) if short else 1
    dmask, q_dec, k_dec = _retention_tables(chunk)
    row = lambda c: pl.BlockSpec((seqs * rows, c), lambda b, i: (b * nc + i, 0))
    state = pl.BlockSpec((seqs, H_R, DK_R, DV_R), lambda b, i: (b, 0, 0, 0))
    return pl.pallas_call(
        functools.partial(_retention_body, chunk_decay=_chunk_decay(chunk)),
        grid=(batch // seqs, nc),
        in_specs=[row(RQ), row(RQ), row(RV), row(RV), state,
                  _resident((H_R, chunk, chunk)), _resident((chunk, RQ)), _resident((chunk, RQ)),
                  _resident((1, RV))],
        out_specs=[row(RV), state],
        out_shape=[jax.ShapeDtypeStruct((n, RV), rv.dtype),
                   jax.ShapeDtypeStruct((batch, H_R, DK_R, DV_R), F32)],
        scratch_shapes=[pltpu.VMEM((seqs, H_R, DK_R, DV_R), F32)],
        name="retention",
    )(rq, rk, rv, rg, s0, dmask, q_dec, k_dec, gn_g)


def _lane_suffix_sum(x):
    lane = lax.broadcasted_iota(jnp.int32, x.shape, 1)
    d = 1
    while d < LANES:
        x = x + jnp.where(lane < LANES - d, pltpu.roll(x, LANES - d, 1), 0.0)
        d *= 2
    return x


def _bf16_pieces(x):
    pieces = []
    for _ in range(BIAS_PIECES):
        piece = x.astype(BF16).astype(F32)
        pieces.append(piece)
        x = x - piece
    return jnp.concatenate(pieces, axis=0).astype(BF16)


def _forget_bias_body(lft_ref, tri_ref, place_ref, o_ref, carry_scr):
    @pl.when(pl.program_id(1) == 0)
    def _():
        carry_scr[...] = jnp.zeros(carry_scr.shape, F32)

    tokens = lft_ref.shape[2]
    c = carry_scr[...] + jnp.dot(_bf16_pieces(lft_ref[0]), tri_ref[...], preferred_element_type=F32
                                 ).reshape(BIAS_PIECES, H_F, tokens).sum(axis=0)
    carry_scr[...] = c[:, tokens - 1:tokens]
    planes = lax.dot_general(_bf16_pieces(c * (-LOG2E)), place_ref[...], _TN,
                             preferred_element_type=F32)
    for j in range(H_F // 2):
        o_ref[0, j] = planes[:, j * 2 * D_F:(j + 1) * 2 * D_F].astype(BF16)


def _bias_placement():
    plane_w = 2 * D_F
    place = np.zeros((BIAS_PIECES, H_F, (H_F // 2) * plane_w), np.float32)
    for i in range(BIAS_PIECES):
        for j in range(H_F // 2):
            place[i, 2 * j, j * plane_w + D_F + i] = 1.0
            place[i, 2 * j + 1, j * plane_w + i] = 1.0
    return jnp.asarray(place.reshape(BIAS_PIECES * H_F, -1), BF16)


def _forget_bias(lft):
    batch, _, length = lft.shape
    tokens = min(BIAS_TOKENS, length)
    nt = length // tokens
    tri = jnp.asarray(np.triu(np.ones((tokens, tokens), np.float32)), BF16)
    place = _bias_placement()
    return pl.pallas_call(
        _forget_bias_body, grid=(batch, nt),
        in_specs=[pl.BlockSpec((1, H_F, tokens), lambda b, i: (b, 0, i)), _resident(tri.shape),
                  _resident(place.shape)],
        out_specs=pl.BlockSpec((1, H_F // 2, tokens, 2 * D_F), lambda b, i: (b, 0, i, 0)),
        out_shape=jax.ShapeDtypeStruct((batch, H_F // 2, length, 2 * D_F), BF16),
        scratch_shapes=[pltpu.VMEM((H_F, 1), F32)],
        name="forget_bias",
    )(lft, tri, place)


def _fox_prefill_body(qi_tab, ki_tab, qt_ref, k_ref, kb_ref, vt_ref, o_ref, qa_scr, m_scr, acc_scr, *, heads):
    step_id = pl.program_id(2)
    qi = qi_tab[step_id]
    ki = ki_tab[step_id]
    pair_w = 2 * D_F
    acc_rows = D_F + BF16_ROWS

    @pl.when(ki == 0)
    def _():
        m_scr[...] = jnp.full(m_scr.shape, NEG_BIG, F32)
        acc_scr[...] = jnp.zeros(acc_scr.shape, F32)
        row = lax.broadcasted_iota(jnp.int32, (pair_w, qt_ref.shape[2]), 0)
        for h in range(heads):
            pair = qt_ref[0, (h // 2) * pair_w:(h // 2 + 1) * pair_w, :]
            first = D_F if h % 2 == 0 else 0
            qa_scr[h] = jnp.where((row >= first) & (row < first + BIAS_PIECES), jnp.ones_like(pair), pair)

    def step(masked):
        tile = k_ref.shape[0]
        half = tile // 2
        lane = lax.broadcasted_iota(jnp.int32, (tile, pair_w), 1)
        ones = jnp.ones((BF16_ROWS, tile), BF16)
        panels = [(0, half, 0, half), (0, tile, half, tile)] if masked else [(0, tile, 0, tile)]
        units = [(h, p) for h in range(heads) for p in panels]
        ka = {}

        def logits(unit):
            h, (k0, k1, q0, q1) = unit
            if h not in ka:
                own = (lane < D_F) if h % 2 == 0 else (lane >= D_F)
                j = h // 2
                ka[h] = jnp.where(own, k_ref[:, j * pair_w:(j + 1) * pair_w], kb_ref[0, j])
            return jnp.dot(ka[h][k0:k1, :], qa_scr[h, :, q0:q1], preferred_element_type=F32)

        lookahead = PREFILL_LOOKAHEAD * len(panels)
        ahead = [logits(u) for u in units[:lookahead]]
        for n, (h, (k0, k1, q0, q1)) in enumerate(units):
            st = ahead.pop(0)
            if n + lookahead < len(units):
                ahead.append(logits(units[n + lookahead]))
            if masked:
                key = k0 + lax.broadcasted_iota(jnp.int32, st.shape, 0)
                qry = q0 + lax.broadcasted_iota(jnp.int32, st.shape, 1)
                st = jnp.where(key <= qry, st, NEG_BIG)
            m_old = m_scr[h, :, q0:q1]
            m_new = jnp.maximum(m_old, jnp.max(st, axis=0, keepdims=True))
            alpha = jnp.exp2(m_old - m_new)
            pt = jnp.exp2(st - m_new)
            va = jnp.concatenate([vt_ref[0, h * D_F:(h + 1) * D_F, k0:k1], ones[:, k0:k1]], axis=0)
            rows = slice(h * acc_rows, (h + 1) * acc_rows)
            acc_scr[rows, q0:q1] = alpha * acc_scr[rows, q0:q1] + jnp.dot(va, pt.astype(BF16),
                                                                          preferred_element_type=F32)
            m_scr[h, :, q0:q1] = m_new

    @pl.when(ki < qi)
    def _():
        step(False)

    @pl.when(ki == qi)
    def _():
        step(True)
        out_t = jnp.concatenate([acc_scr[h * acc_rows:h * acc_rows + D_F, :]
                                 / acc_scr[h * acc_rows + D_F:h * acc_rows + D_F + 1, :] for h in range(heads)],
                                axis=0)
        o_ref[...] = out_t.T.astype(BF16)


def _fox_prefill(fqt, fk, kbias, fvt):
    batch, _, length = fqt.shape
    tile = min(ATTN_TILE, length)
    nt = length // tile
    heads = PREFILL_HEADS
    width = heads * D_F
    pairs = [(q, k) for q in range(nt) for k in range(q + 1)]
    qi_tab = jnp.asarray([q for q, _ in pairs], jnp.int32)
    ki_tab = jnp.asarray([k for _, k in pairs], jnp.int32)
    grid_spec = pltpu.PrefetchScalarGridSpec(
        num_scalar_prefetch=2,
        grid=(batch, H_F // heads, len(pairs)),
        in_specs=[pl.BlockSpec((1, width, tile), lambda b, g, p, qt, kt: (b, g, qt[p])),
                  pl.BlockSpec((tile, width), lambda b, g, p, qt, kt: (b * nt + kt[p], g)),
                  pl.BlockSpec((1, heads // 2, tile, 2 * D_F), lambda b, g, p, qt, kt: (b, g, kt[p], 0)),
                  pl.BlockSpec((1, width, tile), lambda b, g, p, qt, kt: (b, g, kt[p]))],
        out_specs=pl.BlockSpec((tile, width), lambda b, g, p, qt, kt: (b * nt + qt[p], g)),
        scratch_shapes=[pltpu.VMEM((heads, 2 * D_F, tile), BF16), pltpu.VMEM((heads, 1, tile), F32),
                        pltpu.VMEM((heads * (D_F + BF16_ROWS), tile), F32)],
    )
    return pl.pallas_call(
        functools.partial(_fox_prefill_body, heads=heads),
        grid_spec=grid_spec,
        out_shape=jax.ShapeDtypeStruct(fk.shape, BF16),
        name="fox_prefill",
    )(qi_tab, ki_tab, fqt, fk, kbias, fvt)


def _fox_decode_body(pt_ref, q_ref, kn_ref, vn_ref, lfn_ref, k_hbm, v_hbm, lf_hbm, o_ref,
                     k_buf, v_buf, lf_buf, sems, qbd_scr, m_scr, l_scr, acc_scr, r_scr, *, pages, t_new, steps,
                     total):
    b_idx = pl.program_id(0)
    p_idx = pl.program_id(1)
    n_pages = steps * pages
    rows = H_F * t_new
    half = pages // DECODE_GROUPS
    step_id = b_idx * steps + p_idx
    slot = step_id % DECODE_SLOTS

    def page_copies(step):
        b, p, slot = step // steps, step % steps, step % DECODE_SLOTS
        out = []
        for j in range(pages):
            page = pt_ref[b, n_pages - 1 - (p * pages + j)]
            out.append(pltpu.make_async_copy(k_hbm.at[page], k_buf.at[slot, j], sems.at[0, slot]))
            out.append(pltpu.make_async_copy(v_hbm.at[page], v_buf.at[slot, j], sems.at[1, slot]))
            out.append(pltpu.make_async_copy(lf_hbm.at[page], lf_buf.at[slot, j], sems.at[2, slot]))
        return out

    @pl.when(step_id == 0)
    def _():
        for ahead in range(min(DECODE_SLOTS - 1, total)):
            for c in page_copies(step_id + ahead):
                c.start()

    @pl.when(step_id + DECODE_SLOTS - 1 < total)
    def _():
        for c in page_copies(step_id + DECODE_SLOTS - 1):
            c.start()

    def expand(x):
        return jnp.broadcast_to(x[:, None, :], (H_F, t_new, x.shape[-1])).reshape(rows, x.shape[-1])

    @pl.when(p_idx == 0)
    def _():
        q = jnp.concatenate([q_ref[...].astype(F32)] * H_F, axis=0)
        r_head = lax.broadcasted_iota(jnp.int32, (rows, FQ), 0) // t_new
        l_head = lax.broadcasted_iota(jnp.int32, (rows, FQ), 1) // D_F
        qbd = jnp.where(r_head == l_head, q, 0.0).astype(BF16)
        qbd_scr[...] = qbd
        lfn = lfn_ref[0]
        lane = lax.broadcasted_iota(jnp.int32, lfn.shape, 1)
        cnew = jnp.zeros_like(lfn)
        for i in range(t_new):
            cnew = cnew + jnp.where(lane >= i, lfn[:, i:i + 1], 0.0)
        s = lax.dot_general(qbd, kn_ref[...].astype(BF16), _NT, preferred_element_type=F32) - expand(cnew)
        t_q = lax.broadcasted_iota(jnp.int32, s.shape, 0) % t_new
        t_k = lax.broadcasted_iota(jnp.int32, s.shape, 1)
        s = jnp.where(t_k <= t_q, s, NEG_BIG)
        m = jnp.max(s, axis=-1, keepdims=True)
        p = jnp.exp(s - m)
        m_scr[...] = jnp.full(m_scr.shape, NEG_BIG, F32)
        l_scr[...] = jnp.zeros(l_scr.shape, F32)
        acc_scr[...] = jnp.zeros(acc_scr.shape, F32)
        m_scr[0] = m
        l_scr[0] = jnp.sum(p, axis=-1, keepdims=True)
        acc_scr[0] = jnp.dot(p.astype(BF16), vn_ref[...].astype(BF16), preferred_element_type=F32)
        r_scr[...] = jnp.zeros(r_scr.shape, F32)

    for c in page_copies(step_id):
        c.wait()

    qbd = qbd_scr[...]
    run = r_scr[...]
    biases = []
    for j in range(pages):
        lf = lf_buf[slot, j]
        suffix = _lane_suffix_sum(lf)
        biases.append(expand(suffix - lf + run))
        run = run + suffix[:, 0:1]
    r_scr[...] = run
    for g in range(DECODE_GROUPS):
        js = range(g * half, (g + 1) * half)
        s = jnp.concatenate(
            [jnp.dot(qbd, k_buf[slot, j].reshape(FQ, LANES).astype(BF16), preferred_element_type=F32)
             + biases[j] for j in js], axis=-1)
        m_old = m_scr[g]
        m_new = jnp.maximum(m_old, jnp.max(s, axis=-1, keepdims=True))
        alpha = jnp.exp(m_old - m_new)
        p = jnp.exp(s - m_new)
        l_scr[g] = alpha * l_scr[g] + jnp.sum(p, axis=-1, keepdims=True)
        p = p.astype(BF16)
        acc = alpha * acc_scr[g]
        for i, j in enumerate(js):
            acc = acc + lax.dot_general(p[:, i * LANES:(i + 1) * LANES],
                                        v_buf[slot, j].reshape(FQ, LANES).astype(BF16), _NT,
                                        preferred_element_type=F32)
        acc_scr[g] = acc
        m_scr[g] = m_new

    @pl.when(p_idx == steps - 1)
    def _():
        m = jnp.max(m_scr[...], axis=0)
        scale = jnp.exp(m_scr[...] - m)
        full = jnp.sum(scale * acc_scr[...], axis=0) / jnp.sum(scale * l_scr[...], axis=0)
        l_head = lax.broadcasted_iota(jnp.int32, (t_new, FQ), 1) // D_F
        out = jnp.zeros((t_new, FQ), F32)
        for h in range(H_F):
            out = out + jnp.where(l_head == h, full[h * t_new:(h + 1) * t_new, :], 0.0)
        o_ref[...] = out.astype(o_ref.dtype)


def _fox_decode(fq, fk, fv, lf_new, cache_kt, cache_vt, cache_lft, page_table):
    db, n_pages = page_table.shape
    t_new = fq.shape[0] // db
    pages = math.gcd(PAGES_PER_STEP, n_pages)
    assert pages % DECODE_GROUPS == 0
    steps = n_pages // pages
    rows = H_F * t_new
    page_len = cache_kt.shape[-1]
    assert page_len == LANES
    tok = pl.BlockSpec((t_new, FQ), lambda b, p, pt: (b, 0))
    hbm = pl.BlockSpec(memory_space=pltpu.HBM)
    grid_spec = pltpu.PrefetchScalarGridSpec(
        num_scalar_prefetch=1,
        grid=(db, steps),
        in_specs=[tok, tok, tok, pl.BlockSpec((1, H_F, t_new), lambda b, p, pt: (b, 0, 0)), hbm, hbm, hbm],
        out_specs=tok,
        scratch_shapes=[pltpu.VMEM((DECODE_SLOTS, pages, H_F, D_F, page_len), F32),
                        pltpu.VMEM((DECODE_SLOTS, pages, H_F, D_F, page_len), F32),
                        pltpu.VMEM((DECODE_SLOTS, pages, H_F, page_len), F32),
                        pltpu.SemaphoreType.DMA((3, DECODE_SLOTS)),
                        pltpu.VMEM((rows, FQ), BF16),
                        pltpu.VMEM((DECODE_GROUPS, rows, 1), F32), pltpu.VMEM((DECODE_GROUPS, rows, 1), F32),
                        pltpu.VMEM((DECODE_GROUPS, rows, FQ), F32), pltpu.VMEM((H_F, 1), F32)],
    )
    return pl.pallas_call(
        functools.partial(_fox_decode_body, pages=pages, t_new=t_new, steps=steps, total=db * steps),
        grid_spec=grid_spec,
        out_shape=jax.ShapeDtypeStruct(fq.shape, fq.dtype),
        compiler_params=pltpu.CompilerParams(dimension_semantics=("arbitrary", "arbitrary")),
        name="fox_decode",
    )(page_table, fq, fk, fv, lf_new, cache_kt, cache_vt, cache_lft)


def _mem_attn_body(q_ref, k_ref, v_ref, o_ref):
    seqs = k_ref.shape[0]
    tq = q_ref.shape[0] // seqs
    n_mem = k_ref.shape[1] // H_M
    units = [(sq, h) for sq in range(seqs) for h in range(H_M)]
    rows = lambda sq: slice(sq * tq, (sq + 1) * tq)
    lanes = lambda h: slice(h * D_M, (h + 1) * D_M)
    head_rows = lambda h: pl.ds(h, n_mem, stride=H_M)
    scores = [lax.dot_general(q_ref[rows(sq), lanes(h)].astype(BF16), k_ref[sq, head_rows(h), :].astype(BF16),
                              _NT, preferred_element_type=F32) * (D_M ** -0.5) for sq, h in units]
    for (sq, h), s in zip(units, scores):
        p = jnp.exp(s - jnp.max(s, axis=-1, keepdims=True))
        l = jnp.sum(p, axis=-1, keepdims=True)
        o_ref[rows(sq), lanes(h)] = (jnp.dot(p.astype(BF16), v_ref[sq, head_rows(h), :].astype(BF16),
                                             preferred_element_type=F32) / l).astype(o_ref.dtype)


def _mem_attn(mq, mem_k, mem_v):
    batch, mem_rows, _ = mem_k.shape
    length = mq.shape[0] // batch
    tq = min(ATTN_TILE, length)
    nt = length // tq
    seqs = math.gcd(batch, SHORT_SEQS_PER_STEP) if nt == 1 and length * SHORT_SEQS_PER_STEP <= ATTN_TILE else 1
    row = pl.BlockSpec((seqs * tq, MQ), lambda b, i: (b * nt + i, 0))
    mem = pl.BlockSpec((seqs, mem_rows, D_M), lambda b, i: (b, 0, 0))
    return pl.pallas_call(_mem_attn_body, grid=(batch // seqs, nt), in_specs=[row, mem, mem], out_specs=row,
                          out_shape=jax.ShapeDtypeStruct(mq.shape, mq.dtype), name="mem_attn")(mq, mem_k, mem_v)


def _mem_kv_body(x_ref, w_ref, k_ref, v_ref):
    tm = x_ref.shape[0]
    kv = jnp.dot(x_ref[...].astype(BF16), w_ref[...], preferred_element_type=F32)
    for h in range(H_M):
        head_rows = pl.ds(h, tm, stride=H_M)
        k_ref[head_rows, :] = kv[:, h * D_M:(h + 1) * D_M]
        v_ref[head_rows, :] = kv[:, MQ + h * D_M:MQ + (h + 1) * D_M]


def _mem_kv(x, w):
    n, d = x.shape
    tm = min(TOKEN_TILE, n)
    out = pl.BlockSpec((tm * H_M, D_M), lambda i: (i, 0))
    shape = jax.ShapeDtypeStruct((n * H_M, D_M), F32)
    return pl.pallas_call(
        _mem_kv_body, grid=(n // tm,),
        in_specs=[pl.BlockSpec((tm, d), lambda i: (i, 0)), _resident(w.shape)],
        out_specs=[out, out], out_shape=[shape, shape], name="mem_kv_proj")(x, w)


def _merge_body(h_ref, ret_ref, fox_ref, mem_ref, gate_ref, w_ret, w_fox, w_mem, w_out, g_ref, b_ref, o_ref):
    d = h_ref.shape[1]
    n_sub = max(1, h_ref.shape[0] // FFN_SUB_ROWS)
    sub = h_ref.shape[0] // n_sub
    rows = [slice(i * sub, (i + 1) * sub) for i in range(n_sub)]

    def branches(rs):
        return [jnp.dot(x_ref[rs, :].astype(BF16), w_ref[...], preferred_element_type=F32)
                for x_ref, w_ref in ((ret_ref, w_ret), (fox_ref, w_fox), (mem_ref, w_mem))]

    def mixed(rs, br):
        mix = sum(gate_ref[rs, i * d:(i + 1) * d].astype(F32) * b for i, b in enumerate(br))
        return jnp.dot(mix.astype(BF16), w_out[...], preferred_element_type=F32)

    br = branches(rows[0])
    for i, rs in enumerate(rows):
        br_next = branches(rows[i + 1]) if i + 1 < n_sub else None
        o_ref[rs, :] = _layer_norm(ALPHA * h_ref[rs, :] + mixed(rs, br), g_ref[...], b_ref[...])
        br = br_next


def _merge(h, ret, fox, mem, gate, w_ret, w_fox, w_mem, w_out, g, b):
    n, d = h.shape
    tm = min(WIDE_TOKEN_TILE, n)
    row = lambda c: pl.BlockSpec((tm, c), lambda i: (i, 0))
    return pl.pallas_call(
        _merge_body, grid=(n // tm,),
        in_specs=[row(d), row(RV), row(FQ), row(MQ), row(N_BRANCH * d),
                  _resident(w_ret.shape), _resident(w_fox.shape), _resident(w_mem.shape), _resident(w_out.shape),
                  _resident((1, d)), _resident((1, d))],
        out_specs=row(d),
        out_shape=jax.ShapeDtypeStruct((n, d), F32), name="merge",
    )(h, ret, fox, mem, gate, w_ret, w_fox, w_mem, w_out, g, b)


def _rope_tables(pos):
    half = DK_R // 2
    inv = ROPE_BASE ** (-jnp.arange(half, dtype=F32) / half)
    ang = pos[:, None] * inv[None, :]
    cos, sin = jnp.cos(ang), jnp.sin(ang)
    return (jnp.tile(jnp.concatenate([cos, cos], axis=1), (1, H_R)),
            jnp.tile(jnp.concatenate([-sin, sin], axis=1), (1, H_R)))


def _split_w_in(w_in, d_model, prefill):
    sizes = (RQ, RQ, RV, RV, FQ, FQ, FQ, H_F, MQ, N_BRANCH * d_model)
    names = ("rq", "rk", "rv", "rg", "fq", "fk", "fv", "fl", "mq", "gl")
    offs = np.concatenate([[0], np.cumsum(sizes)])
    w = {nm: w_in[:, int(offs[i]):int(offs[i + 1])].astype(BF16) for i, nm in enumerate(names)}
    if prefill:
        w["fq"], w["fv"] = w["fq"].T, w["fv"].T
    w["fl"] = w["fl"].T
    return w


def _layer(x, pos, groups, s0, mem_k, mem_v, fox_attend, prefill, p):
    d = x.shape[1]
    ln_g, ln_b = p["ln_g"], p["ln_b"]
    act_dtype = BF16 if (x.shape[0] // s0.shape[0]) % BF16_ROWS == 0 else F32
    h = _ffn_ln(x, p["wg"], p["wu"], p["wd"], 0, ln_g[0:1], ln_b[0:1])
    cos, sin = _rope_tables(pos)
    rq, rk, rv, rg, mq, gl, *fox_in = _in_proj(
        h, cos, sin, _split_w_in(p["w_in"], d, prefill), p["b_forget"], p["b_gate"], groups, prefill, act_dtype)
    ret, ret_state = _retention(rq, rk, rv, rg, s0, p["ret_gn_g"])
    fox = fox_attend(*fox_in)
    mem = _mem_attn(mq, mem_k, mem_v)
    h = _merge(h, ret, fox, mem, gl, p["w_ret_o"], p["w_fox_o"], p["w_mem_o"], p["w_out"], ln_g[1:2], ln_b[1:2])
    y = _ffn_ln(h, p["wg"], p["wu"], p["wd"], 1, ln_g[2:3], ln_b[2:3])
    return y, fox_in, ret_state


def kernel(x_prompt, x_sample, mem_prompt, cache_fox_k, cache_fox_v, cache_fox_logf, state_ret, cache_mem_k, cache_mem_v, page_table, ln_g, ln_b, w_ffn_gate, w_ffn_up, w_ffn_down, w_in, b_forget, b_gate, ret_gn_g, w_ret_o, w_fox_o, w_mem_k, w_mem_v, w_mem_o, w_out):
    bp, lp, d = x_prompt.shape
    db, ls, _ = x_sample.shape
    n_mem = mem_prompt.shape[1]
    n_pages = page_table.shape[1]
    page = cache_fox_k.shape[1]
    past_len = n_pages * page
    params = {
        "ln_g": ln_g, "ln_b": ln_b,
        "wg": w_ffn_gate.astype(BF16), "wu": w_ffn_up.astype(BF16), "wd": w_ffn_down.astype(BF16),
        "w_in": w_in, "b_forget": b_forget.reshape(H_F, 1), "b_gate": b_gate,
        "ret_gn_g": ret_gn_g.reshape(1, RV),
        "w_ret_o": w_ret_o.astype(BF16), "w_fox_o": w_fox_o.astype(BF16), "w_mem_o": w_mem_o.astype(BF16),
        "w_out": w_out.astype(BF16),
    }

    mem_k_p, mem_v_p = (t.reshape(bp, n_mem * H_M, D_M) for t in _mem_kv(
        mem_prompt.reshape(bp * n_mem, d), jnp.concatenate([w_mem_k, w_mem_v], axis=1).astype(BF16)))

    def attend_prompt(fqt, fk, fkt, fvt, fvt16, lft):
        return _fox_prefill(fqt, fk, _forget_bias(lft), fvt16)

    y_p, (_, _, fkt_p, fvt_p, _, lft_p), ret_p = _layer(
        x_prompt.reshape(bp * lp, d), jnp.arange(lp, dtype=F32), bp, jnp.zeros((bp, H_R, DK_R, DV_R), F32),
        mem_k_p, mem_v_p, attend_prompt, True, params)

    cache_kt = jnp.transpose(cache_fox_k, (0, 2, 3, 1))
    cache_vt = jnp.transpose(cache_fox_v, (0, 2, 3, 1))
    cache_lft = jnp.transpose(cache_fox_logf, (0, 2, 1))

    def attend_sample(fq, fk, fv, lft):
        lf_new = jnp.transpose(lft.reshape(H_F, db, ls), (1, 0, 2))
        return _fox_decode(fq, fk, fv, lf_new, cache_kt, cache_vt, cache_lft, page_table)

    y_s, (_, fk_s, fv_s, lft_s), ret_s = _layer(
        x_sample.reshape(db * ls, d), past_len + jnp.tile(jnp.arange(ls, dtype=F32), db), 1, state_ret,
        cache_mem_k.reshape(db, n_mem * H_M, D_M), cache_mem_v.reshape(db, n_mem * H_M, D_M), attend_sample,
        False, params)

    to_heads = lambda t, b, l: jnp.transpose(t.reshape(b, H_F, D_F, l), (0, 3, 1, 2))
    return (y_p.reshape(bp, lp, d), y_s.reshape(db, ls, d),
            to_heads(fkt_p, bp, lp), to_heads(fvt_p, bp, lp), jnp.transpose(lft_p, (0, 2, 1)),
            ret_p, mem_k_p.reshape(bp, n_mem, H_M, D_M), mem_v_p.reshape(bp, n_mem, H_M, D_M),
            fk_s.reshape(db, ls, H_F, D_F), fv_s.reshape(db, ls, H_F, D_F),
            jnp.transpose(lft_s.reshape(H_F, db, ls), (1, 2, 0)), ret_s)
```

```python
import functools
import math

import jax
import jax.numpy as jnp
import numpy as np
from jax import lax
from jax.experimental import pallas as pl
from jax.experimental.pallas import tpu as pltpu

F32 = jnp.float32
BF16 = jnp.bfloat16

DEPTH = 1
H_R, DK_R, DV_R = 4, 64, 128
RET_CHUNK = 128
RET_CHUNKS_PER_STEP = 4
SHORT_SEQS_PER_STEP = 8
ROPE_BASE = 10000.0
H_F, D_F = 8, 64
H_M, D_M = 4, 128
N_BRANCH = 3
LN_EPS = 1e-5
GN_EPS = 1e-5
ALPHA = (2.0 * DEPTH) ** 0.25
RQ, RV, FQ, MQ = H_R * DK_R, H_R * DV_R, H_F * D_F, H_M * D_M

LANES = 128
BF16_ROWS = 16
TOKEN_TILE = 256
WIDE_TOKEN_TILE = 1024
FFN_SUB_ROWS = 256
ATTN_TILE = 1024
PREFILL_TILE = 1024
PREFILL_LOOKAHEAD = 2
PREFILL_HEADS = 8
PAGES_PER_STEP = 16
DECODE_GROUPS = 2
DMA_PRIORITIES = 2
DECODE_SLOTS = 3
NEG_BIG = -1e30
LOG2E = math.log2(math.e)
BIAS_TOKENS = 1024
BIAS_PIECES = 3

_NT = (((1,), (1,)), ((), ()))
_TN = (((0,), (0,)), ((), ()))


def _resident(shape):
    return pl.BlockSpec(shape, lambda *_: (0,) * len(shape), pipeline_mode=pl.Buffered(1))


def _layer_norm(y, g, b):
    mu = jnp.mean(y, axis=-1, keepdims=True)
    yc = y - mu
    var = jnp.mean(yc * yc, axis=-1, keepdims=True)
    return yc * lax.rsqrt(var + LN_EPS) * g + b


def _silu(x):
    return x * jax.nn.sigmoid(x)


def _ffn_ln_body(x_ref, wg_ref, wu_ref, wd_ref, g_ref, b_ref, o_ref):
    n_sub = max(1, x_ref.shape[0] // FFN_SUB_ROWS)
    rows = [slice(i * (x_ref.shape[0] // n_sub), (i + 1) * (x_ref.shape[0] // n_sub)) for i in range(n_sub)]

    def gate_up(rs):
        xb = x_ref[rs, :].astype(BF16)
        return (jnp.dot(xb, wg_ref[...], preferred_element_type=F32),
                jnp.dot(xb, wu_ref[...], preferred_element_type=F32))

    def down(gu):
        return jnp.dot((_silu(gu[0]) * gu[1]).astype(BF16), wd_ref[...], preferred_element_type=F32)

    gu = gate_up(rows[0])
    f_prev = None
    for i, rs in enumerate(rows):
        gu_next = gate_up(rows[i + 1]) if i + 1 < n_sub else None
        f = down(gu)
        if f_prev is not None:
            o_ref[rows[i - 1], :] = _layer_norm(ALPHA * x_ref[rows[i - 1], :] + 0.5 * f_prev, g_ref[...], b_ref[...])
        gu, f_prev = gu_next, f
    o_ref[rows[-1], :] = _layer_norm(ALPHA * x_ref[rows[-1], :] + 0.5 * f_prev, g_ref[...], b_ref[...])


def _ffn_ln(x, wg, wu, wd, layer, g, b):
    n, d = x.shape
    d_ff = wg.shape[2]
    tm = min(WIDE_TOKEN_TILE, n)
    stacked = lambda r, c: pl.BlockSpec((None, r, c), lambda i: (layer, 0, 0), pipeline_mode=pl.Buffered(1))
    return pl.pallas_call(
        _ffn_ln_body,
        grid=(n // tm,),
        in_specs=[pl.BlockSpec((tm, d), lambda i: (i, 0)),
                  stacked(d, d_ff), stacked(d, d_ff), stacked(d_ff, d),
                  _resident((1, d)), _resident((1, d))],
        out_specs=pl.BlockSpec((tm, d), lambda i: (i, 0)),
        out_shape=jax.ShapeDtypeStruct((n, d), F32),
        name="ffn_ln",
    )(x, wg, wu, wd, g, b)


def _rotary(x, cos, sin_signed):
    width = x.shape[-1]
    half = DK_R // 2
    lane = lax.broadcasted_iota(jnp.int32, x.shape, 1)
    other = jnp.where((lane % DK_R) < half,
                      pltpu.roll(x, width - half, 1),
                      pltpu.roll(x, half, 1))
    return x * cos + other * sin_signed


def _log_sigmoid(x):
    return jnp.minimum(x, 0.0) - jnp.log1p(jnp.exp(-jnp.abs(x)))


def _in_proj_body(h_ref, cos_ref, sin_ref, w_rq, w_rk, w_rv, w_rg, w_mq, w_gl, w_fq, w_fk, w_fv, w_fl, bf_ref,
                  bg_ref, *rest, prefill):
    rq_o, rk_o, rv_o, rg_o, mq_o, gl_o = rest[:6]
    fox_o = rest[6:]
    hb = h_ref[...].astype(BF16)

    def proj(w):
        return jnp.dot(hb, w, preferred_element_type=F32)

    def proj_t(wt):
        return lax.dot_general(wt, hb, _NT, preferred_element_type=F32)

    cos, sin = cos_ref[...], sin_ref[...]
    rq_o[...] = _rotary(proj(w_rq[...]), cos, sin)
    rk_o[...] = _rotary(proj(w_rk[...]), cos, sin) * (DK_R ** -0.5)
    rv_o[...] = proj(w_rv[...]).astype(rv_o.dtype)
    rg_o[...] = proj(w_rg[...])
    mq_o[...] = proj(w_mq[...]).astype(mq_o.dtype)
    gl_o[...] = jax.nn.sigmoid(proj(w_gl[...]) + bg_ref[...]).astype(gl_o.dtype)
    if prefill:
        fqt_o, fk_o, fkt_o, fvt_o, fvt16_o, lft_o = fox_o
        fqt_o[0] = (proj_t(w_fq[...]) * (D_F ** -0.5 * LOG2E)).astype(BF16)
        fk = proj(w_fk[...])
        fk_o[...] = fk.astype(BF16)
        fkt_o[0] = fk.T
        fvt = proj_t(w_fv[...])
        fvt_o[0] = fvt
        fvt16_o[0] = fvt.astype(BF16)
        lft_o[0] = _log_sigmoid(proj_t(w_fl[...]) + bf_ref[...])
    else:
        fq_o, fk_o, fv_o, lft_o = fox_o
        fq_o[...] = (proj(w_fq[...]) * (D_F ** -0.5)).astype(fq_o.dtype)
        fk_o[...] = proj(w_fk[...])
        fv_o[...] = proj(w_fv[...])
        lft_o[0] = _log_sigmoid(proj_t(w_fl[...]) + bf_ref[...])


def _in_proj(h, cos, sin, w, b_forget, b_gate, groups, prefill, act_dtype):
    n, d = h.shape
    lg = n // groups
    tm = min(TOKEN_TILE, lg)
    nt = lg // tm
    row = lambda c: pl.BlockSpec((tm, c), lambda g, i: (g * nt + i, 0))
    tab = pl.BlockSpec((tm, RQ), lambda g, i: (i, 0))
    colt = lambda c: pl.BlockSpec((1, c, tm), lambda g, i: (g, 0, i))
    rows = lambda c, dt: jax.ShapeDtypeStruct((n, c), dt)
    cols = lambda c, dt: jax.ShapeDtypeStruct((groups, c, lg), dt)
    if prefill:
        fox_specs = [colt(FQ), row(FQ), colt(FQ), colt(FQ), colt(FQ), colt(H_F)]
        fox_shapes = [cols(FQ, BF16), rows(FQ, BF16), cols(FQ, F32), cols(FQ, F32), cols(FQ, BF16),
                      cols(H_F, F32)]
    else:
        fox_specs = [row(FQ), row(FQ), row(FQ), colt(H_F)]
        fox_shapes = [rows(FQ, act_dtype), rows(FQ, F32), rows(FQ, F32), cols(H_F, F32)]
    gates = w["gl"].shape[1]
    weights = [w[k] for k in ("rq", "rk", "rv", "rg", "mq", "gl", "fq", "fk", "fv", "fl")]
    weights += [b_forget, b_gate.reshape(1, gates)]
    return pl.pallas_call(
        functools.partial(_in_proj_body, prefill=prefill),
        grid=(groups, nt),
        in_specs=[row(d), tab, tab] + [_resident(a.shape) for a in weights],
        out_specs=[row(RQ), row(RQ), row(RV), row(RV), row(MQ), row(gates)] + fox_specs,
        out_shape=[rows(RQ, F32), rows(RQ, F32), rows(RV, act_dtype), rows(RV, F32), rows(MQ, act_dtype),
                   rows(gates, BF16)] + fox_shapes,
        name="in_proj",
    )(h, cos, sin, *weights)


def _retention_body(rq_ref, rk_ref, rv_ref, rg_ref, s0_ref, dmask_ref, qdec_ref, kdec_ref, gn_ref,
                    o_ref, sfin_ref, s_scr, *, chunk_decay):
    c = pl.program_id(1)

    @pl.when(c == 0)
    def _():
        s_scr[...] = s0_ref[...]

    chunk = dmask_ref.shape[1]
    seqs = s_scr.shape[0]
    seq_rows = rq_ref.shape[0] // seqs
    n_sub = seq_rows // chunk
    ks = lambda h: slice(h * DK_R, (h + 1) * DK_R)
    vs = lambda h: slice(h * DV_R, (h + 1) * DV_R)
    rs = lambda sq, i: slice(sq * seq_rows + i * chunk, sq * seq_rows + (i + 1) * chunk)
    units = [(sq, h, i) for sq in range(seqs) for h in range(H_R) for i in range(n_sub)]
    decayed = {u: lax.dot_general(rq_ref[rs(u[0], u[2]), ks(u[1])].astype(BF16),
                                  rk_ref[rs(u[0], u[2]), ks(u[1])].astype(BF16), _NT,
                                  preferred_element_type=F32) * dmask_ref[u[1]] for u in units}
    intra, kv = {}, {}
    for sq, h, i in units:
        vh = rv_ref[rs(sq, i), vs(h)].astype(BF16)
        intra[sq, h, i] = jnp.dot(decayed[sq, h, i].astype(BF16), vh, preferred_element_type=F32)
        kv[sq, h, i] = lax.dot_general((rk_ref[rs(sq, i), ks(h)] * kdec_ref[:, ks(h)]).astype(BF16), vh, _TN,
                                       preferred_element_type=F32)
    for sq, h in [(sq, h) for sq in range(seqs) for h in range(H_R)]:
        s_h = s_scr[sq, h]
        for i in range(n_sub):
            o = intra[sq, h, i] + jnp.dot((rq_ref[rs(sq, i), ks(h)] * qdec_ref[:, ks(h)]).astype(BF16),
                                          s_h.astype(BF16), preferred_element_type=F32)
            s_h = s_h * chunk_decay[h] + kv[sq, h, i]
            mu = jnp.mean(o, axis=-1, keepdims=True)
            oc = o - mu
            var = jnp.mean(oc * oc, axis=-1, keepdims=True)
            o_ref[rs(sq, i), vs(h)] = (_silu(rg_ref[rs(sq, i), vs(h)])
                                       * (oc * lax.rsqrt(var + GN_EPS) * gn_ref[:, vs(h)])).astype(o_ref.dtype)
        s_scr[sq, h] = s_h

    @pl.when(c == pl.num_programs(1) - 1)
    def _():
        sfin_ref[...] = s_scr[...]


def _retention_tables(chunk):
    log_g = jnp.log1p(-jnp.exp2(-5.0 - jnp.arange(H_R, dtype=F32)))
    i = jnp.arange(chunk, dtype=F32)
    diff = i[:, None] - i[None, :]
    dmask = jnp.where(diff[None] >= 0, jnp.exp(jnp.maximum(diff, 0.0)[None] * log_g[:, None, None]), 0.0)
    q_dec = jnp.exp((i + 1.0)[:, None] * log_g[None, :])
    k_dec = jnp.exp((chunk - 1.0 - i)[:, None] * log_g[None, :])
    widen = lambda t: jnp.repeat(t, DK_R, axis=1)
    return dmask, widen(q_dec), widen(k_dec)


def _chunk_decay(chunk):
    log_g = np.log1p(-np.exp2(-5.0 - np.arange(H_R, dtype=np.float32))).astype(np.float32)
    return tuple(float(v) for v in np.exp(np.float32(chunk) * log_g).astype(np.float32))


def _retention(rq, rk, rv, rg, s0, gn_g):
    batch = s0.shape[0]
    n = rq.shape[0]
    length = n // batch
    chunk = math.gcd(length, RET_CHUNK)
    rows = chunk * math.gcd(length // chunk, RET_CHUNKS_PER_STEP)
    nc = length // rows
    short = nc == 1 and length * SHORT_SEQS_PER_STEP <= RET_CHUNK * RET_CHUNKS_PER_STEP
    seqs = math.gcd(batch, SHORT_SEQS_PER_STEP) if short else 1
    dmask, q_dec, k_dec = _retention_tables(chunk)
    row = lambda c: pl.BlockSpec((seqs * rows, c), lambda b, i: (b * nc + i, 0))
    state = pl.BlockSpec((seqs, H_R, DK_R, DV_R), lambda b, i: (b, 0, 0, 0))
    return pl.pallas_call(
        functools.partial(_retention_body, chunk_decay=_chunk_decay(chunk)),
        grid=(batch // seqs, nc),
        in_specs=[row(RQ), row(RQ), row(RV), row(RV), state,
                  _resident((H_R, chunk, chunk)), _resident((chunk, RQ)), _resident((chunk, RQ)),
                  _resident((1, RV))],
        out_specs=[row(RV), state],
        out_shape=[jax.ShapeDtypeStruct((n, RV), rv.dtype),
                   jax.ShapeDtypeStruct((batch, H_R, DK_R, DV_R), F32)],
        scratch_shapes=[pltpu.VMEM((seqs, H_R, DK_R, DV_R), F32)],
        name="retention",
    )(rq, rk, rv, rg, s0, dmask, q_dec, k_dec, gn_g)


def _lane_suffix_sum(x):
    lane = lax.broadcasted_iota(jnp.int32, x.shape, 1)
    d = 1
    while d < LANES:
        x = x + jnp.where(lane < LANES - d, pltpu.roll(x, LANES - d, 1), 0.0)
        d *= 2
    return x


def _bf16_pieces(x):
    pieces = []
    for _ in range(BIAS_PIECES):
        piece = x.astype(BF16).astype(F32)
        pieces.append(piece)
        x = x - piece
    return jnp.concatenate(pieces, axis=0).astype(BF16)


def _forget_bias_body(lft_ref, tri_ref, place_ref, o_ref, carry_scr):
    @pl.when(pl.program_id(1) == 0)
    def _():
        carry_scr[...] = jnp.zeros(carry_scr.shape, F32)

    tokens = lft_ref.shape[2]
    c = carry_scr[...] + jnp.dot(_bf16_pieces(lft_ref[0]), tri_ref[...], preferred_element_type=F32
                                 ).reshape(BIAS_PIECES, H_F, tokens).sum(axis=0)
    carry_scr[...] = c[:, tokens - 1:tokens]
    planes = lax.dot_general(_bf16_pieces(c * (-LOG2E)), place_ref[...], _TN,
                             preferred_element_type=F32)
    for j in range(H_F // 2):
        o_ref[0, j] = planes[:, j * 2 * D_F:(j + 1) * 2 * D_F].astype(BF16)


def _bias_placement():
    plane_w = 2 * D_F
    place = np.zeros((BIAS_PIECES, H_F, (H_F // 2) * plane_w), np.float32)
    for i in range(BIAS_PIECES):
        for j in range(H_F // 2):
            place[i, 2 * j, j * plane_w + D_F + i] = 1.0
            place[i, 2 * j + 1, j * plane_w + i] = 1.0
    return jnp.asarray(place.reshape(BIAS_PIECES * H_F, -1), BF16)


def _forget_bias(lft):
    batch, _, length = lft.shape
    tokens = min(BIAS_TOKENS, length)
    nt = length // tokens
    tri = jnp.asarray(np.triu(np.ones((tokens, tokens), np.float32)), BF16)
    place = _bias_placement()
    return pl.pallas_call(
        _forget_bias_body, grid=(batch, nt),
        in_specs=[pl.BlockSpec((1, H_F, tokens), lambda b, i: (b, 0, i)), _resident(tri.shape),
                  _resident(place.shape)],
        out_specs=pl.BlockSpec((1, H_F // 2, tokens, 2 * D_F), lambda b, i: (b, 0, i, 0)),
        out_shape=jax.ShapeDtypeStruct((batch, H_F // 2, length, 2 * D_F), BF16),
        scratch_shapes=[pltpu.VMEM((H_F, 1), F32)],
        name="forget_bias",
    )(lft, tri, place)


def _fox_prefill_body(qi_tab, ki_tab, qt_ref, k_ref, kb_ref, vt_ref, o_ref, qa_scr, m_scr, acc_scr, *, heads):
    step_id = pl.program_id(2)
    qi = qi_tab[step_id]
    ki = ki_tab[step_id]
    pair_w = 2 * D_F
    acc_rows = D_F + BF16_ROWS

    @pl.when(ki == 0)
    def _():
        m_scr[...] = jnp.full(m_scr.shape, NEG_BIG, F32)
        acc_scr[...] = jnp.zeros(acc_scr.shape, F32)
        row = lax.broadcasted_iota(jnp.int32, (pair_w, qt_ref.shape[2]), 0)
        for h in range(heads):
            pair = qt_ref[0, (h // 2) * pair_w:(h // 2 + 1) * pair_w, :]
            first = D_F if h % 2 == 0 else 0
            qa_scr[h] = jnp.where((row >= first) & (row < first + BIAS_PIECES), jnp.ones_like(pair), pair)

    def step(masked):
        tile = k_ref.shape[0]
        half = tile // 2
        lane = lax.broadcasted_iota(jnp.int32, (tile, pair_w), 1)
        ones = jnp.ones((BF16_ROWS, tile), BF16)
        panels = [(0, half, 0, half), (0, tile, half, tile)] if masked else [(0, tile, 0, tile)]
        units = [(h, p) for h in range(heads) for p in panels]
        ka = {}

        def logits(unit):
            h, (k0, k1, q0, q1) = unit
            if h not in ka:
                own = (lane < D_F) if h % 2 == 0 else (lane >= D_F)
                j = h // 2
                ka[h] = jnp.where(own, k_ref[:, j * pair_w:(j + 1) * pair_w], kb_ref[0, j])
            return jnp.dot(ka[h][k0:k1, :], qa_scr[h, :, q0:q1], preferred_element_type=F32)

        lookahead = PREFILL_LOOKAHEAD * len(panels)
        ahead = [logits(u) for u in units[:lookahead]]
        for n, (h, (k0, k1, q0, q1)) in enumerate(units):
            st = ahead.pop(0)
            if n + lookahead < len(units):
                ahead.append(logits(units[n + lookahead]))
            if masked:
                key = k0 + lax.broadcasted_iota(jnp.int32, st.shape, 0)
                qry = q0 + lax.broadcasted_iota(jnp.int32, st.shape, 1)
                st = jnp.where(key <= qry, st, NEG_BIG)
            m_old = m_scr[h, :, q0:q1]
            m_new = jnp.maximum(m_old, jnp.max(st, axis=0, keepdims=True))
            alpha = jnp.exp2(m_old - m_new)
            pt = jnp.exp2(st - m_new)
            va = jnp.concatenate([vt_ref[0, h * D_F:(h + 1) * D_F, k0:k1], ones[:, k0:k1]], axis=0)
            rows = slice(h * acc_rows, (h + 1) * acc_rows)
            acc_scr[rows, q0:q1] = alpha * acc_scr[rows, q0:q1] + jnp.dot(va, pt.astype(BF16),
                                                                          preferred_element_type=F32)
            m_scr[h, :, q0:q1] = m_new

    @pl.when(ki < qi)
    def _():
        step(False)

    @pl.when(ki == qi)
    def _():
        step(True)
        out_t = jnp.concatenate([acc_scr[h * acc_rows:h * acc_rows + D_F, :]
                                 / acc_scr[h * acc_rows + D_F:h * acc_rows + D_F + 1, :] for h in range(heads)],
                                axis=0)
        o_ref[...] = out_t.T.astype(BF16)


def _fox_prefill(fqt, fk, kbias, fvt):
    batch, _, length = fqt.shape
    tile = min(PREFILL_TILE, length)
    nt = length // tile
    heads = PREFILL_HEADS
    width = heads * D_F
    pairs = [(q, k) for q in range(nt) for k in range(q + 1)]
    qi_tab = jnp.asarray([q for q, _ in pairs], jnp.int32)
    ki_tab = jnp.asarray([k for _, k in pairs], jnp.int32)
    grid_spec = pltpu.PrefetchScalarGridSpec(
        num_scalar_prefetch=2,
        grid=(batch, H_F // heads, len(pairs)),
        in_specs=[pl.BlockSpec((1, width, tile), lambda b, g, p, qt, kt: (b, g, qt[p])),
                  pl.BlockSpec((tile, width), lambda b, g, p, qt, kt: (b * nt + kt[p], g)),
                  pl.BlockSpec((1, heads // 2, tile, 2 * D_F), lambda b, g, p, qt, kt: (b, g, kt[p], 0)),
                  pl.BlockSpec((1, width, tile), lambda b, g, p, qt, kt: (b, g, kt[p]))],
        out_specs=pl.BlockSpec((tile, width), lambda b, g, p, qt, kt: (b * nt + qt[p], g)),
        scratch_shapes=[pltpu.VMEM((heads, 2 * D_F, tile), BF16), pltpu.VMEM((heads, 1, tile), F32),
                        pltpu.VMEM((heads * (D_F + BF16_ROWS), tile), F32)],
    )
    return pl.pallas_call(
        functools.partial(_fox_prefill_body, heads=heads),
        grid_spec=grid_spec,
        out_shape=jax.ShapeDtypeStruct(fk.shape, BF16),
        name="fox_prefill",
    )(qi_tab, ki_tab, fqt, fk, kbias, fvt)


def _fox_decode_body(pt_ref, q_ref, kn_ref, vn_ref, lfn_ref, k_hbm, v_hbm, lf_hbm, o_ref,
                     k_buf, v_buf, lf_buf, sems, qbd_scr, m_scr, l_scr, acc_scr, r_scr, *, pages, t_new, steps,
                     total):
    b_idx = pl.program_id(0)
    p_idx = pl.program_id(1)
    n_pages = steps * pages
    rows = H_F * t_new
    half = pages // DECODE_GROUPS
    step_id = b_idx * steps + p_idx
    slot = step_id % DECODE_SLOTS

    def page_copies(step):
        b, p, slot = step // steps, step % steps, step % DECODE_SLOTS
        out = []
        for j in range(pages):
            page = pt_ref[b, n_pages - 1 - (p * pages + j)]
            out.append(pltpu.make_async_copy(k_hbm.at[page], k_buf.at[slot, j], sems.at[0, slot]))
            out.append(pltpu.make_async_copy(v_hbm.at[page], v_buf.at[slot, j], sems.at[1, slot]))
            out.append(pltpu.make_async_copy(lf_hbm.at[page], lf_buf.at[slot, j], sems.at[2, slot]))
        return out

    @pl.when(step_id == 0)
    def _():
        for ahead in range(min(DECODE_SLOTS - 1, total)):
            for i, c in enumerate(page_copies(step_id + ahead)):
                c.start(priority=i % DMA_PRIORITIES)

    @pl.when(step_id + DECODE_SLOTS - 1 < total)
    def _():
        for i, c in enumerate(page_copies(step_id + DECODE_SLOTS - 1)):
            c.start(priority=i % DMA_PRIORITIES)

    def expand(x):
        return jnp.broadcast_to(x[:, None, :], (H_F, t_new, x.shape[-1])).reshape(rows, x.shape[-1])

    @pl.when(p_idx == 0)
    def _():
        q = jnp.concatenate([q_ref[...].astype(F32)] * H_F, axis=0)
        r_head = lax.broadcasted_iota(jnp.int32, (rows, FQ), 0) // t_new
        l_head = lax.broadcasted_iota(jnp.int32, (rows, FQ), 1) // D_F
        qbd = jnp.where(r_head == l_head, q, 0.0).astype(BF16)
        qbd_scr[...] = qbd
        lfn = lfn_ref[0]
        lane = lax.broadcasted_iota(jnp.int32, lfn.shape, 1)
        cnew = jnp.zeros_like(lfn)
        for i in range(t_new):
            cnew = cnew + jnp.where(lane >= i, lfn[:, i:i + 1], 0.0)
        s = lax.dot_general(qbd, kn_ref[...].astype(BF16), _NT, preferred_element_type=F32) - expand(cnew)
        t_q = lax.broadcasted_iota(jnp.int32, s.shape, 0) % t_new
        t_k = lax.broadcasted_iota(jnp.int32, s.shape, 1)
        s = jnp.where(t_k <= t_q, s, NEG_BIG)
        m = jnp.max(s, axis=-1, keepdims=True)
        p = jnp.exp(s - m)
        m_scr[...] = jnp.full(m_scr.shape, NEG_BIG, F32)
        l_scr[...] = jnp.zeros(l_scr.shape, F32)
        acc_scr[...] = jnp.zeros(acc_scr.shape, F32)
        m_scr[0] = m
        l_scr[0] = jnp.sum(p, axis=-1, keepdims=True)
        acc_scr[0] = jnp.dot(p.astype(BF16), vn_ref[...].astype(BF16), preferred_element_type=F32)
        r_scr[...] = jnp.zeros(r_scr.shape, F32)

    for c in page_copies(step_id):
        c.wait()

    qbd = qbd_scr[...]
    run = r_scr[...]
    biases = []
    for j in range(pages):
        lf = lf_buf[slot, j]
        suffix = _lane_suffix_sum(lf)
        biases.append(expand(suffix - lf + run))
        run = run + suffix[:, 0:1]
    r_scr[...] = run
    for g in range(DECODE_GROUPS):
        js = range(g * half, (g + 1) * half)
        s = jnp.concatenate(
            [jnp.dot(qbd, k_buf[slot, j].reshape(FQ, LANES).astype(BF16), preferred_element_type=F32)
             + biases[j] for j in js], axis=-1)
        m_old = m_scr[g]
        m_new = jnp.maximum(m_old, jnp.max(s, axis=-1, keepdims=True))
        alpha = jnp.exp(m_old - m_new)
        p = jnp.exp(s - m_new)
        l_scr[g] = alpha * l_scr[g] + jnp.sum(p, axis=-1, keepdims=True)
        p = p.astype(BF16)
        acc = alpha * acc_scr[g]
        for i, j in enumerate(js):
            acc = acc + lax.dot_general(p[:, i * LANES:(i + 1) * LANES],
                                        v_buf[slot, j].reshape(FQ, LANES).astype(BF16), _NT,
                                        preferred_element_type=F32)
        acc_scr[g] = acc
        m_scr[g] = m_new

    @pl.when(p_idx == steps - 1)
    def _():
        m = jnp.max(m_scr[...], axis=0)
        scale = jnp.exp(m_scr[...] - m)
        full = jnp.sum(scale * acc_scr[...], axis=0) / jnp.sum(scale * l_scr[...], axis=0)
        l_head = lax.broadcasted_iota(jnp.int32, (t_new, FQ), 1) // D_F
        out = jnp.zeros((t_new, FQ), F32)
        for h in range(H_F):
            out = out + jnp.where(l_head == h, full[h * t_new:(h + 1) * t_new, :], 0.0)
        o_ref[...] = out.astype(o_ref.dtype)


def _fox_decode(fq, fk, fv, lf_new, cache_kt, cache_vt, cache_lft, page_table):
    db, n_pages = page_table.shape
    t_new = fq.shape[0] // db
    pages = math.gcd(PAGES_PER_STEP, n_pages)
    assert pages % DECODE_GROUPS == 0
    steps = n_pages // pages
    rows = H_F * t_new
    page_len = cache_kt.shape[-1]
    assert page_len == LANES
    tok = pl.BlockSpec((t_new, FQ), lambda b, p, pt: (b, 0))
    hbm = pl.BlockSpec(memory_space=pltpu.HBM)
    grid_spec = pltpu.PrefetchScalarGridSpec(
        num_scalar_prefetch=1,
        grid=(db, steps),
        in_specs=[tok, tok, tok, pl.BlockSpec((1, H_F, t_new), lambda b, p, pt: (b, 0, 0)), hbm, hbm, hbm],
        out_specs=tok,
        scratch_shapes=[pltpu.VMEM((DECODE_SLOTS, pages, H_F, D_F, page_len), F32),
                        pltpu.VMEM((DECODE_SLOTS, pages, H_F, D_F, page_len), F32),
                        pltpu.VMEM((DECODE_SLOTS, pages, H_F, page_len), F32),
                        pltpu.SemaphoreType.DMA((3, DECODE_SLOTS)),
                        pltpu.VMEM((rows, FQ), BF16),
                        pltpu.VMEM((DECODE_GROUPS, rows, 1), F32), pltpu.VMEM((DECODE_GROUPS, rows, 1), F32),
                        pltpu.VMEM((DECODE_GROUPS, rows, FQ), F32), pltpu.VMEM((H_F, 1), F32)],
    )
    return pl.pallas_call(
        functools.partial(_fox_decode_body, pages=pages, t_new=t_new, steps=steps, total=db * steps),
        grid_spec=grid_spec,
        out_shape=jax.ShapeDtypeStruct(fq.shape, fq.dtype),
        compiler_params=pltpu.CompilerParams(dimension_semantics=("arbitrary", "arbitrary")),
        name="fox_decode",
    )(page_table, fq, fk, fv, lf_new, cache_kt, cache_vt, cache_lft)


def _mem_attn_body(q_ref, k_ref, v_ref, o_ref):
    seqs = k_ref.shape[0]
    tq = q_ref.shape[0] // seqs
    n_mem = k_ref.shape[1] // H_M
    units = [(sq, h) for sq in range(seqs) for h in range(H_M)]
    rows = lambda sq: slice(sq * tq, (sq + 1) * tq)
    lanes = lambda h: slice(h * D_M, (h + 1) * D_M)
    head_rows = lambda h: pl.ds(h, n_mem, stride=H_M)
    scores = [lax.dot_general(q_ref[rows(sq), lanes(h)].astype(BF16), k_ref[sq, head_rows(h), :].astype(BF16),
                              _NT, preferred_element_type=F32) * (D_M ** -0.5) for sq, h in units]
    for (sq, h), s in zip(units, scores):
        p = jnp.exp(s - jnp.max(s, axis=-1, keepdims=True))
        l = jnp.sum(p, axis=-1, keepdims=True)
        o_ref[rows(sq), lanes(h)] = (jnp.dot(p.astype(BF16), v_ref[sq, head_rows(h), :].astype(BF16),
                                             preferred_element_type=F32) / l).astype(o_ref.dtype)


def _mem_attn(mq, mem_k, mem_v):
    batch, mem_rows, _ = mem_k.shape
    length = mq.shape[0] // batch
    tq = min(ATTN_TILE, length)
    nt = length // tq
    seqs = math.gcd(batch, SHORT_SEQS_PER_STEP) if nt == 1 and length * SHORT_SEQS_PER_STEP <= ATTN_TILE else 1
    row = pl.BlockSpec((seqs * tq, MQ), lambda b, i: (b * nt + i, 0))
    mem = pl.BlockSpec((seqs, mem_rows, D_M), lambda b, i: (b, 0, 0))
    return pl.pallas_call(_mem_attn_body, grid=(batch // seqs, nt), in_specs=[row, mem, mem], out_specs=row,
                          out_shape=jax.ShapeDtypeStruct(mq.shape, mq.dtype), name="mem_attn")(mq, mem_k, mem_v)


def _mem_kv_body(x_ref, w_ref, k_ref, v_ref):
    tm = x_ref.shape[0]
    kv = jnp.dot(x_ref[...].astype(BF16), w_ref[...], preferred_element_type=F32)
    for h in range(H_M):
        head_rows = pl.ds(h, tm, stride=H_M)
        k_ref[head_rows, :] = kv[:, h * D_M:(h + 1) * D_M]
        v_ref[head_rows, :] = kv[:, MQ + h * D_M:MQ + (h + 1) * D_M]


def _mem_kv(x, w):
    n, d = x.shape
    tm = min(TOKEN_TILE, n)
    out = pl.BlockSpec((tm * H_M, D_M), lambda i: (i, 0))
    shape = jax.ShapeDtypeStruct((n * H_M, D_M), F32)
    return pl.pallas_call(
        _mem_kv_body, grid=(n // tm,),
        in_specs=[pl.BlockSpec((tm, d), lambda i: (i, 0)), _resident(w.shape)],
        out_specs=[out, out], out_shape=[shape, shape], name="mem_kv_proj")(x, w)


def _merge_body(h_ref, ret_ref, fox_ref, mem_ref, gate_ref, w_ret, w_fox, w_mem, w_out, g_ref, b_ref, o_ref):
    d = h_ref.shape[1]
    n_sub = max(1, h_ref.shape[0] // FFN_SUB_ROWS)
    sub = h_ref.shape[0] // n_sub
    rows = [slice(i * sub, (i + 1) * sub) for i in range(n_sub)]

    def branches(rs):
        return [jnp.dot(x_ref[rs, :].astype(BF16), w_ref[...], preferred_element_type=F32)
                for x_ref, w_ref in ((ret_ref, w_ret), (fox_ref, w_fox), (mem_ref, w_mem))]

    def mixed(rs, br):
        mix = sum(gate_ref[rs, i * d:(i + 1) * d].astype(F32) * b for i, b in enumerate(br))
        return jnp.dot(mix.astype(BF16), w_out[...], preferred_element_type=F32)

    br = branches(rows[0])
    for i, rs in enumerate(rows):
        br_next = branches(rows[i + 1]) if i + 1 < n_sub else None
        o_ref[rs, :] = _layer_norm(ALPHA * h_ref[rs, :] + mixed(rs, br), g_ref[...], b_ref[...])
        br = br_next


def _merge(h, ret, fox, mem, gate, w_ret, w_fox, w_mem, w_out, g, b):
    n, d = h.shape
    tm = min(WIDE_TOKEN_TILE, n)
    row = lambda c: pl.BlockSpec((tm, c), lambda i: (i, 0))
    return pl.pallas_call(
        _merge_body, grid=(n // tm,),
        in_specs=[row(d), row(RV), row(FQ), row(MQ), row(N_BRANCH * d),
                  _resident(w_ret.shape), _resident(w_fox.shape), _resident(w_mem.shape), _resident(w_out.shape),
                  _resident((1, d)), _resident((1, d))],
        out_specs=row(d),
        out_shape=jax.ShapeDtypeStruct((n, d), F32), name="merge",
    )(h, ret, fox, mem, gate, w_ret, w_fox, w_mem, w_out, g, b)


def _rope_tables(pos):
    half = DK_R // 2
    inv = ROPE_BASE ** (-jnp.arange(half, dtype=F32) / half)
    ang = pos[:, None] * inv[None, :]
    cos, sin = jnp.cos(ang), jnp.sin(ang)
    return (jnp.tile(jnp.concatenate([cos, cos], axis=1), (1, H_R)),
            jnp.tile(jnp.concatenate([-sin, sin], axis=1), (1, H_R)))


def _split_w_in(w_in, d_model, prefill):
    sizes = (RQ, RQ, RV, RV, FQ, FQ, FQ, H_F, MQ, N_BRANCH * d_model)
    names = ("rq", "rk", "rv", "rg", "fq", "fk", "fv", "fl", "mq", "gl")
    offs = np.concatenate([[0], np.cumsum(sizes)])
    w = {nm: w_in[:, int(offs[i]):int(offs[i + 1])].astype(BF16) for i, nm in enumerate(names)}
    if prefill:
        w["fq"], w["fv"] = w["fq"].T, w["fv"].T
    w["fl"] = w["fl"].T
    return w


def _layer(x, pos, groups, s0, mem_k, mem_v, fox_attend, prefill, p):
    d = x.shape[1]
    ln_g, ln_b = p["ln_g"], p["ln_b"]
    act_dtype = BF16 if (x.shape[0] // s0.shape[0]) % BF16_ROWS == 0 else F32
    h = _ffn_ln(x, p["wg"], p["wu"], p["wd"], 0, ln_g[0:1], ln_b[0:1])
    cos, sin = _rope_tables(pos)
    rq, rk, rv, rg, mq, gl, *fox_in = _in_proj(
        h, cos, sin, _split_w_in(p["w_in"], d, prefill), p["b_forget"], p["b_gate"], groups, prefill, act_dtype)
    ret, ret_state = _retention(rq, rk, rv, rg, s0, p["ret_gn_g"])
    fox = fox_attend(*fox_in)
    mem = _mem_attn(mq, mem_k, mem_v)
    h = _merge(h, ret, fox, mem, gl, p["w_ret_o"], p["w_fox_o"], p["w_mem_o"], p["w_out"], ln_g[1:2], ln_b[1:2])
    y = _ffn_ln(h, p["wg"], p["wu"], p["wd"], 1, ln_g[2:3], ln_b[2:3])
    return y, fox_in, ret_state


def kernel(x_prompt, x_sample, mem_prompt, cache_fox_k, cache_fox_v, cache_fox_logf, state_ret, cache_mem_k, cache_mem_v, page_table, ln_g, ln_b, w_ffn_gate, w_ffn_up, w_ffn_down, w_in, b_forget, b_gate, ret_gn_g, w_ret_o, w_fox_o, w_mem_k, w_mem_v, w_mem_o, w_out):
    bp, lp, d = x_prompt.shape
    db, ls, _ = x_sample.shape
    n_mem = mem_prompt.shape[1]
    n_pages = page_table.shape[1]
    page = cache_fox_k.shape[1]
    past_len = n_pages * page
    params = {
        "ln_g": ln_g, "ln_b": ln_b,
        "wg": w_ffn_gate.astype(BF16), "wu": w_ffn_up.astype(BF16), "wd": w_ffn_down.astype(BF16),
        "w_in": w_in, "b_forget": b_forget.reshape(H_F, 1), "b_gate": b_gate,
        "ret_gn_g": ret_gn_g.reshape(1, RV),
        "w_ret_o": w_ret_o.astype(BF16), "w_fox_o": w_fox_o.astype(BF16), "w_mem_o": w_mem_o.astype(BF16),
        "w_out": w_out.astype(BF16),
    }

    mem_k_p, mem_v_p = (t.reshape(bp, n_mem * H_M, D_M) for t in _mem_kv(
        mem_prompt.reshape(bp * n_mem, d), jnp.concatenate([w_mem_k, w_mem_v], axis=1).astype(BF16)))

    def attend_prompt(fqt, fk, fkt, fvt, fvt16, lft):
        return _fox_prefill(fqt, fk, _forget_bias(lft), fvt16)

    y_p, (_, _, fkt_p, fvt_p, _, lft_p), ret_p = _layer(
        x_prompt.reshape(bp * lp, d), jnp.arange(lp, dtype=F32), bp, jnp.zeros((bp, H_R, DK_R, DV_R), F32),
        mem_k_p, mem_v_p, attend_prompt, True, params)

    cache_kt = jnp.transpose(cache_fox_k, (0, 2, 3, 1))
    cache_vt = jnp.transpose(cache_fox_v, (0, 2, 3, 1))
    cache_lft = jnp.transpose(cache_fox_logf, (0, 2, 1))

    def attend_sample(fq, fk, fv, lft):
        lf_new = jnp.transpose(lft.reshape(H_F, db, ls), (1, 0, 2))
        return _fox_decode(fq, fk, fv, lf_new, cache_kt, cache_vt, cache_lft, page_table)

    y_s, (_, fk_s, fv_s, lft_s), ret_s = _layer(
        x_sample.reshape(db * ls, d), past_len + jnp.tile(jnp.arange(ls, dtype=F32), db), 1, state_ret,
        cache_mem_k.reshape(db, n_mem * H_M, D_M), cache_mem_v.reshape(db, n_mem * H_M, D_M), attend_sample,
        False, params)

    to_heads = lambda t, b, l: jnp.transpose(t.reshape(b, H_F, D_F, l), (0, 3, 1, 2))
    return (y_p.reshape(bp, lp, d), y_s.reshape(db, ls, d),
            to_heads(fkt_p, bp, lp), to_heads(fvt_p, bp, lp), jnp.transpose(lft_p, (0, 2, 1)),
            ret_p, mem_k_p.reshape(bp, n_mem, H_M, D_M), mem_v_p.reshape(bp, n_mem, H_M, D_M),
            fk_s.reshape(db, ls, H_F, D_F), fv_s.reshape(db, ls, H_F, D_F),
            jnp.transpose(lft_s.reshape(H_F, db, ls), (1, 2, 0)), ret_s)
```
